```python
import math
import jax
import jax.numpy as jnp
from jax import lax
import numpy as np

D_MODEL = 1024
BATCH = 4
SEQ = 4096
DEPTH = 2
DEC_BATCH = 32
DEC_SEQ = 1
PAST_LEN = 16384
PAGE_SIZE = 128

N_BRANCH = 4
BRANCH_W = D_MODEL // N_BRANCH
DA_HEADS = 4
DA_DV = BRANCH_W // DA_HEADS
DA_DQK = DA_DV // 2
A_QK = DA_HEADS * 2 * DA_DQK
A_V = DA_HEADS * DA_DV
SC_W = BRANCH_W
CONV_K = 3
MB_HEADS = 4
MB_DH = BRANCH_W // MB_HEADS
MB_W = MB_HEADS * MB_DH
MOBA_BLOCK = 256
MOBA_TOPK = 3
MOBA_QCHUNK = 64
CH_GROUPS = 4
CH_DG = BRANCH_W // CH_GROUPS
CH_W = CH_GROUPS * CH_DG
CHUNK = 128
N_EGROUPS = 4
E_PER_GROUP = 8
E_TOPK = 2
D_EXPERT = D_MODEL // 4
ATTN_QBLOCK = 128
ROPE_THETA = 10000.0
NORM_EPS = 1e-6
NEG_INF = -1e30
IN_SPLITS = (A_QK, A_QK, A_V, SC_W, SC_W, SC_W, MB_W, MB_W, MB_W, CH_W, CH_W, N_BRANCH * D_MODEL)
IN_COLS = sum(IN_SPLITS)

kernel_name = 'hybrid_diffattn_conv_moba_chunkmlp_hmoe_step'


def _split_points():
    return [int(p) for p in np.cumsum(IN_SPLITS)[:-1]]


def rms_norm(x, g):
    xf = x.astype(jnp.float32)
    y = xf * lax.rsqrt(jnp.mean(xf * xf, axis=-1, keepdims=True) + NORM_EPS)
    return (y * g.astype(jnp.float32)).astype(x.dtype)


def rope(x, pos):
    d = x.shape[-1]
    half = d // 2
    inv = jnp.exp(-math.log(ROPE_THETA) * jnp.arange(half, dtype=jnp.float32) * (2.0 / d))
    ang = pos.astype(jnp.float32)[:, None] * inv[None, :]
    shape = (1, x.shape[1]) + (1,) * (x.ndim - 3) + (half,)
    c = jnp.cos(ang).reshape(shape)
    s = jnp.sin(ang).reshape(shape)
    xf = x.astype(jnp.float32)
    x1, x2 = xf[..., :half], xf[..., half:]
    return jnp.concatenate([x1 * c - x2 * s, x1 * s + x2 * c], axis=-1).astype(x.dtype)


def diff_attention(q, k, v, lam, q0):
    B_, Sq, H, _, d = q.shape
    Sk = k.shape[1]
    qb = ATTN_QBLOCK if Sq % ATTN_QBLOCK == 0 else Sq
    nq = Sq // qb
    qr = q.reshape(B_, nq, qb, H, 2, d).transpose(1, 0, 2, 3, 4, 5)
    kpos = jnp.arange(Sk, dtype=jnp.int32)
    scale = DA_DQK ** -0.5

    def block(args):
        qblk, start = args
        s = jnp.einsum('bqhmd,bkhmd->bhmqk', qblk, k, preferred_element_type=jnp.float32) * scale
        qpos = q0 + start + jnp.arange(qb, dtype=jnp.int32)
        mask = kpos[None, :] <= qpos[:, None]
        p = jax.nn.softmax(jnp.where(mask, s, NEG_INF), axis=-1)
        pd = p[:, :, 0] - lam * p[:, :, 1]
        return jnp.einsum('bhqk,bkhd->bqhd', pd.astype(v.dtype), v)

    starts = jnp.arange(nq, dtype=jnp.int32) * qb
    o = lax.map(block, (qr, starts))
    return o.transpose(1, 0, 2, 3, 4).reshape(B_, Sq, H, v.shape[-1])


def moba_attention(q, k, v, q0):
    B_, Sq, H, dh = q.shape
    Sk = k.shape[1]
    nb = -(-Sk // MOBA_BLOCK)
    pad = nb * MOBA_BLOCK - Sk
    kb = jnp.pad(k, ((0, 0), (0, pad), (0, 0), (0, 0))).reshape(B_, nb, MOBA_BLOCK, H, dh).transpose(0, 3, 1, 2, 4)
    vb = jnp.pad(v, ((0, 0), (0, pad), (0, 0), (0, 0))).reshape(B_, nb, MOBA_BLOCK, H, dh).transpose(0, 3, 1, 2, 4)
    kmean = jnp.mean(kb.astype(jnp.float32), axis=3)
    topk = min(MOBA_TOPK, nb)
    qc = MOBA_QCHUNK if Sq % MOBA_QCHUNK == 0 else Sq
    nq = Sq // qc
    qr = q.reshape(B_, nq, qc, H, dh).transpose(1, 0, 3, 2, 4)
    gather = jax.vmap(jax.vmap(lambda blk, ix: blk[ix]))
    scale = MB_DH ** -0.5
    blk_ids = jnp.arange(nb, dtype=jnp.int32)
    in_blk = jnp.arange(MOBA_BLOCK, dtype=jnp.int32)

    def chunk(args):
        qblk, start = args
        qpos = q0 + start + jnp.arange(qc, dtype=jnp.int32)
        own = qpos // MOBA_BLOCK
        gs = jnp.einsum('bhqd,bhnd->bhqn', qblk.astype(jnp.float32), kmean)
        gs = jnp.where(blk_ids[None, :] < own[:, None], gs, NEG_INF)
        _, top = lax.top_k(gs, topk)
        valid = top < own[None, None, :, None]
        own_b = jnp.broadcast_to(own[None, None, :, None], (B_, H, qc, 1))
        idx = jnp.concatenate([top, own_b], axis=-1)
        slot_ok = jnp.concatenate([valid, jnp.ones((B_, H, qc, 1), dtype=bool)], axis=-1)
        kg = gather(kb, idx)
        vg = gather(vb, idx)
        s = jnp.einsum('bhqd,bhqnkd->bhqnk', qblk, kg, preferred_element_type=jnp.float32) * scale
        kpos = idx[..., None] * MOBA_BLOCK + in_blk
        mask = (kpos <= qpos[None, None, :, None, None]) & slot_ok[..., None]
        s = jnp.where(mask, s, NEG_INF)
        p = jax.nn.softmax(s.reshape(B_, H, qc, -1), axis=-1).reshape(s.shape)
        return jnp.einsum('bhqnk,bhqnkd->bhqd', p.astype(vg.dtype), vg)

    starts = jnp.arange(nq, dtype=jnp.int32) * qc
    o = lax.map(chunk, (qr, starts))
    return o.transpose(1, 0, 3, 2, 4).reshape(B_, Sq, H * dh)


def chunk_mix(u, v, ws, bs):
    B_, S, G, dg = v.shape
    nc = -(-S // CHUNK)
    pad = nc * CHUNK - S
    vp = jnp.pad(v, ((0, 0), (0, pad), (0, 0), (0, 0))).reshape(B_, nc, CHUNK, G, dg)
    causal = jnp.tril(jnp.ones((CHUNK, CHUNK), dtype=ws.dtype))
    mixed = jnp.einsum('gts,bnsgc->bntgc', ws * causal, vp) + bs.T[:, :, None]
    mixed = mixed.reshape(B_, nc * CHUNK, G, dg)[:, :S]
    return u * mixed


def token_mixers(h, l, q0, W, past):
    B_, S, _ = h.shape
    proj = jnp.einsum('bsd,dc->bsc', h, W['w_in'][l])
    aq, ak, av, sb, sc, sh, mq, mk, mv, du, dv, gl = jnp.split(proj, _split_points(), axis=-1)
    pos = q0 + jnp.arange(S, dtype=jnp.int32)
    aq = rope(aq.reshape(B_, S, DA_HEADS, 2, DA_DQK), pos)
    ak = rope(ak.reshape(B_, S, DA_HEADS, 2, DA_DQK), pos).reshape(B_, S, DA_HEADS, 2 * DA_DQK)
    av = av.reshape(B_, S, DA_HEADS, DA_DV)
    mq = rope(mq.reshape(B_, S, MB_HEADS, MB_DH), pos)
    mk = rope(mk.reshape(B_, S, MB_HEADS, MB_DH), pos)
    mv = mv.reshape(B_, S, MB_HEADS, MB_DH)
    if past is None:
        kd_all, vd_all, km_all, vm_all = ak, av, mk, mv
        conv_state = jnp.zeros((B_, CONV_K - 1, SC_W), h.dtype)
    else:
        pkd, pvd, pkm, pvm, conv_state = past
        kd_all = jnp.concatenate([pkd, ak], axis=1)
        vd_all = jnp.concatenate([pvd, av], axis=1)
        km_all = jnp.concatenate([pkm, mk], axis=1)
        vm_all = jnp.concatenate([pvm, mv], axis=1)
    lam_init = 0.8 - 0.6 * math.exp(-0.3 * l)
    f32 = jnp.float32
    lam = (jnp.exp(jnp.sum(W['lambda_q1'][l].astype(f32) * W['lambda_k1'][l].astype(f32)))
           - jnp.exp(jnp.sum(W['lambda_q2'][l].astype(f32) * W['lambda_k2'][l].astype(f32))) + lam_init)
    oa = diff_attention(aq, kd_all.reshape(B_, -1, DA_HEADS, 2, DA_DQK), vd_all, lam, q0)
    oa = (rms_norm(oa, W['subln_g'][l]) * (1.0 - lam_init)).reshape(B_, S, A_V)
    z = sc * sh
    zfull = jnp.concatenate([conv_state.astype(z.dtype), z], axis=1)
    cw = W['conv_w'][l]
    y = cw[0] * zfull[:, 0:S] + cw[1] * zfull[:, 1:S + 1] + cw[2] * zfull[:, 2:S + 2]
    new_conv = zfull[:, -(CONV_K - 1):]
    ob = sb * y
    oc = moba_attention(mq, km_all, vm_all, q0)
    od = chunk_mix(du.reshape(B_, S, CH_GROUPS, CH_DG), dv.reshape(B_, S, CH_GROUPS, CH_DG),
                   W['chunk_ws'][l], W['chunk_b'][l]).reshape(B_, S, CH_W)
    branches = jnp.stack([oa, ob, oc, od], axis=2)
    pb = jnp.einsum('bskc,kcd->bskd', branches, W['w_branch'][l])
    gates = jax.nn.sigmoid(gl.reshape(B_, S, N_BRANCH, D_MODEL))
    merged = jnp.sum(gates * pb, axis=2)
    out = jnp.einsum('bsd,de->bse', merged, W['w_out'][l])
    return out, (ak, av, mk, mv, new_conv, dv)


def hier_moe(x, l, W):
    f32 = jnp.float32
    glog = jnp.einsum('nd,dg->ng', x, W['router_g_w'][l], preferred_element_type=f32) + W['router_g_b'][l].astype(f32)
    gprob = jax.nn.softmax(glog, axis=-1)
    gsel = jnp.argmax(glog, axis=-1)
    gp = jnp.take_along_axis(gprob, gsel[:, None], axis=1)[:, 0]
    elog = jnp.einsum('nd,gde->nge', x, W['router_e_w'][l], preferred_element_type=f32) + W['router_e_b'][l].astype(f32)
    elog = jnp.take_along_axis(elog, gsel[:, None, None], axis=1)[:, 0]
    ev, ei = lax.top_k(elog, E_TOPK)
    ew = jax.nn.softmax(ev, axis=-1) * gp[:, None]
    ce = jnp.sum(jax.nn.one_hot(ei, E_PER_GROUP, dtype=f32) * ew[..., None], axis=1)
    comb = (jax.nn.one_hot(gsel, N_EGROUPS, dtype=f32)[:, :, None] * ce[:, None, :]).astype(x.dtype)
    out = jnp.zeros_like(x)
    for g in range(N_EGROUPS):
        a = (jax.nn.silu(jnp.einsum('nd,edf->nef', x, W['w_gate_e'][l][g]))
             * jnp.einsum('nd,edf->nef', x, W['w_up_e'][l][g]))
        out = out + jnp.einsum('nef,efd->nd', a * comb[:, g, :, None], W['w_down_e'][l][g])
    return out


def run_stack(x, q0, pasts, W):
    states = []
    for l in range(DEPTH):
        past = None if pasts is None else pasts[l]
        m, st = token_mixers(rms_norm(x, W['norm_mix_g'][l]), l, q0, W, past)
        x = x + m
        h = rms_norm(x, W['norm_ffn_g'][l])
        x = x + hier_moe(h.reshape(-1, D_MODEL), l, W).reshape(x.shape)
        states.append(st)
    return rms_norm(x, W['norm_final_g']), states


def gather_pages(cache_l, page_table):
    pages = cache_l[page_table]
    return pages.reshape(page_table.shape[0], -1, cache_l.shape[-2], cache_l.shape[-1])


def setup_inputs(seed: int = 0) -> dict:
    key = jax.random.key(seed)
    ks = jax.random.split(key, 32)
    f32 = jnp.float32
    n_pages = PAST_LEN // PAGE_SIZE
    n_used = DEC_BATCH * n_pages
    n_pool = n_used + n_used // 4

    def nrm(k, shape, s):
        return jax.random.normal(k, shape, f32) * s

    page_table = jax.random.permutation(ks[7], n_pool)[:n_used].reshape(DEC_BATCH, n_pages).astype(jnp.int32)
    return {
        'x_prompt': nrm(ks[0], (BATCH, SEQ, D_MODEL), 1.0),
        'x_sample': nrm(ks[1], (DEC_BATCH, DEC_SEQ, D_MODEL), 1.0),
        'cache_k_diff': nrm(ks[2], (DEPTH, n_pool, PAGE_SIZE, DA_HEADS, 2 * DA_DQK), 1.0),
        'cache_v_diff': nrm(ks[3], (DEPTH, n_pool, PAGE_SIZE, DA_HEADS, DA_DV), 1.0),
        'cache_k_moba': nrm(ks[4], (DEPTH, n_pool, PAGE_SIZE, MB_HEADS, MB_DH), 1.0),
        'cache_v_moba': nrm(ks[5], (DEPTH, n_pool, PAGE_SIZE, MB_HEADS, MB_DH), 1.0),
        'state_conv': nrm(ks[6], (DEPTH, DEC_BATCH, CONV_K - 1, SC_W), 1.0),
        'page_table': page_table,
        'norm_mix_g': 1.0 + nrm(ks[8], (DEPTH, D_MODEL), 0.02),
        'w_in': nrm(ks[9], (DEPTH, D_MODEL, IN_COLS), D_MODEL ** -0.5),
        'lambda_q1': nrm(ks[10], (DEPTH, DA_DQK), 0.1),
        'lambda_k1': nrm(ks[11], (DEPTH, DA_DQK), 0.1),
        'lambda_q2': nrm(ks[12], (DEPTH, DA_DQK), 0.1),
        'lambda_k2': nrm(ks[13], (DEPTH, DA_DQK), 0.1),
        'subln_g': 1.0 + nrm(ks[14], (DEPTH, DA_DV), 0.02),
        'conv_w': nrm(ks[15], (DEPTH, CONV_K, SC_W), CONV_K ** -0.5),
        'chunk_ws': nrm(ks[16], (DEPTH, CH_GROUPS, CHUNK, CHUNK), CHUNK ** -0.5),
        'chunk_b': 1.0 + nrm(ks[17], (DEPTH, CH_GROUPS, CHUNK), 0.1),
        'w_branch': nrm(ks[18], (DEPTH, N_BRANCH, BRANCH_W, D_MODEL), BRANCH_W ** -0.5),
        'w_out': nrm(ks[19], (DEPTH, D_MODEL, D_MODEL), D_MODEL ** -0.5),
        'norm_ffn_g': 1.0 + nrm(ks[20], (DEPTH, D_MODEL), 0.02),
        'router_g_w': nrm(ks[21], (DEPTH, D_MODEL, N_EGROUPS), D_MODEL ** -0.5),
        'router_g_b': nrm(ks[22], (DEPTH, N_EGROUPS), 0.01),
        'router_e_w': nrm(ks[23], (DEPTH, N_EGROUPS, D_MODEL, E_PER_GROUP), D_MODEL ** -0.5),
        'router_e_b': nrm(ks[24], (DEPTH, N_EGROUPS, E_PER_GROUP), 0.01),
        'w_gate_e': nrm(ks[25], (DEPTH, N_EGROUPS, E_PER_GROUP, D_MODEL, D_EXPERT), D_MODEL ** -0.5),
        'w_up_e': nrm(ks[26], (DEPTH, N_EGROUPS, E_PER_GROUP, D_MODEL, D_EXPERT), D_MODEL ** -0.5),
        'w_down_e': nrm(ks[27], (DEPTH, N_EGROUPS, E_PER_GROUP, D_EXPERT, D_MODEL), D_EXPERT ** -0.5),
        'norm_final_g': 1.0 + nrm(ks[28], (D_MODEL,), 0.02),
    }


def reference(x_prompt, x_sample, cache_k_diff, cache_v_diff, cache_k_moba, cache_v_moba, state_conv,
              page_table, norm_mix_g, w_in, lambda_q1, lambda_k1, lambda_q2, lambda_k2, subln_g, conv_w,
              chunk_ws, chunk_b, w_branch, w_out, norm_ffn_g, router_g_w, router_g_b, router_e_w,
              router_e_b, w_gate_e, w_up_e, w_down_e, norm_final_g):
    W = {
        'norm_mix_g': norm_mix_g, 'w_in': w_in, 'lambda_q1': lambda_q1, 'lambda_k1': lambda_k1,
        'lambda_q2': lambda_q2, 'lambda_k2': lambda_k2, 'subln_g': subln_g, 'conv_w': conv_w,
        'chunk_ws': chunk_ws, 'chunk_b': chunk_b, 'w_branch': w_branch, 'w_out': w_out,
        'norm_ffn_g': norm_ffn_g, 'router_g_w': router_g_w, 'router_g_b': router_g_b,
        'router_e_w': router_e_w, 'router_e_b': router_e_b, 'w_gate_e': w_gate_e, 'w_up_e': w_up_e,
        'w_down_e': w_down_e, 'norm_final_g': norm_final_g,
    }
    y_prompt, sp = run_stack(x_prompt, 0, None, W)
    pasts = [(gather_pages(cache_k_diff[l], page_table), gather_pages(cache_v_diff[l], page_table),
              gather_pages(cache_k_moba[l], page_table), gather_pages(cache_v_moba[l], page_table),
              state_conv[l]) for l in range(DEPTH)]
    y_sample, ss = run_stack(x_sample, PAST_LEN, pasts, W)
    return (y_prompt, y_sample,
            jnp.stack([s[0] for s in sp]), jnp.stack([s[1] for s in sp]),
            jnp.stack([s[2] for s in sp]), jnp.stack([s[3] for s in sp]),
            jnp.stack([s[4] for s in sp]),
            jnp.stack([s[0] for s in ss]), jnp.stack([s[1] for s in ss]),
            jnp.stack([s[2] for s in ss]), jnp.stack([s[3] for s in ss]),
            jnp.stack([s[4] for s in ss]), jnp.stack([s[5] for s in ss]))
```

```python
import functools
import math

import jax
import jax.numpy as jnp
from jax import lax
from jax.experimental import pallas as pl
from jax.experimental.pallas import tpu as pltpu

F32 = jnp.float32
BF16 = jnp.bfloat16

D_MODEL = 1024
BRANCH_W = 256
N_BRANCH = 4
DA_HEADS, DA_DQK = 4, 32
MB_HEADS, MB_DH = 4, 64
MOBA_BLOCK = 256
MOBA_TOPK = 3
CHUNK = 128
CH_GROUPS = 4
N_EGROUPS, E_PER_GROUP = 4, 8
N_EXPERTS = N_EGROUPS * E_PER_GROUP
D_EXPERT = 256
PAGE_SIZE = 128
ROPE_THETA = 10000.0
NORM_EPS = 1e-6
NEG_INF = -1e30
N_QKV_COLS = 11 * BRANCH_W

LANES = 128
SUBLANES = 8
VMEM_LIMIT = 48 * 1024 * 1024


def _params(sem):
    return pltpu.CompilerParams(dimension_semantics=sem, vmem_limit_bytes=VMEM_LIMIT)


def _rms(x, g):
    return x * lax.rsqrt(jnp.mean(x * x, axis=-1, keepdims=True) + NORM_EPS) * g


def _dot(a, b):
    return jnp.dot(a, b, preferred_element_type=F32)


def _dot_nt(a, b):
    return lax.dot_general(a, b, (((1,), (1,)), ((), ())), preferred_element_type=F32)


def _dot_f32(a, b):
    return jnp.dot(a, b, precision=lax.Precision.HIGHEST, preferred_element_type=F32)


def _split_bf16(x):
    hi = x.astype(BF16)
    return hi, (x - hi.astype(F32)).astype(BF16)


def _rope_table_kernel(ca_ref, sa_ref, cm_ref, sm_ref, *, pos0, step, rows):
    i = pl.program_id(0)
    r = lax.broadcasted_iota(jnp.int32, (rows, BRANCH_W), 0)
    c = lax.broadcasted_iota(jnp.int32, (rows, BRANCH_W), 1)
    pos = (pos0 + (i * rows + r) * step).astype(F32)

    def tables(d):
        half = d // 2
        ci = c & (d - 1)
        fi = (ci & (half - 1)).astype(F32)
        inv = jnp.exp((-math.log(ROPE_THETA) * fi) * (2.0 / d))
        ang = pos * inv
        sign = jnp.where(ci < half, -1.0, 1.0).astype(F32)
        return jnp.cos(ang), jnp.sin(ang) * sign

    ca_ref[...], sa_ref[...] = tables(DA_DQK)
    cm_ref[...], sm_ref[...] = tables(MB_DH)


def _rope_tables(n_rows, pos0, step):
    rows = min(n_rows, 512)
    spec = pl.BlockSpec((rows, BRANCH_W), lambda i: (i, 0))
    shp = jax.ShapeDtypeStruct((n_rows, BRANCH_W), F32)
    return pl.pallas_call(
        functools.partial(_rope_table_kernel, pos0=pos0, step=step, rows=rows),
        grid=(n_rows // rows,), out_specs=[spec] * 4, out_shape=[shp] * 4,
        compiler_params=_params(("parallel",)), name="rope_tables")()


def _rope(x, cos, sin_signed, d):
    half = d // 2
    c = lax.broadcasted_iota(jnp.int32, x.shape, 1)
    first = (c & (d - 1)) < half
    rot = jnp.where(first, pltpu.roll(x, BRANCH_W - half, 1), pltpu.roll(x, half, 1))
    return x * cos + rot * sin_signed


def _inproj_kernel(*refs, tm, tiles_per_seq, decode):
    if decode:
        (x_ref, g_ref, w_ref, ca_ref, sa_ref, cm_ref, sm_ref, cw_ref, wd_ref, b0_ref, s0_ref, s1_ref,
         ak_o, av_o, mk_o, mv_o, dv_o, z_o, qa_o, qm_o, ob_o, od_o) = refs
    else:
        (x_ref, g_ref, w_ref, ca_ref, sa_ref, cm_ref, sm_ref, cw_ref, wsc_ref, bsf_ref,
         ak_o, av_o, mk_o, mv_o, conv_o, qa_o, ka_o, va_o, qm_o, km_o, vm_o, ob_o, od_o, zbuf) = refs
    i = pl.program_id(0)
    act = F32 if decode else BF16
    hb = _rms(x_ref[...], g_ref[...]).astype(act)

    def proj(k):
        w = w_ref[:, k * BRANCH_W:(k + 1) * BRANCH_W]
        return _dot_f32(hb, w) if decode else _dot(hb, w)

    ca, sa, cm, sm = ca_ref[...], sa_ref[...], cm_ref[...], sm_ref[...]
    aq = _rope(proj(0), ca, sa, DA_DQK)
    ak = _rope(proj(1), ca, sa, DA_DQK)
    av = proj(2)
    ak_o[...] = ak
    av_o[...] = av
    qa_o[...] = (aq * (DA_DQK ** -0.5)).astype(act)
    mq = _rope(proj(6), cm, sm, MB_DH)
    mk = _rope(proj(7), cm, sm, MB_DH)
    mv = proj(8)
    mk_o[...] = mk
    mv_o[...] = mv
    qm_o[...] = (mq * (MB_DH ** -0.5)).astype(act)
    if not decode:
        ka_o[...] = ak.astype(BF16)
        va_o[...] = av.astype(BF16)
        km_o[...] = mk.astype(BF16)
        vm_o[...] = mv.astype(BF16)
    sb = proj(3)
    z = proj(4) * proj(5)
    cw = cw_ref[...]
    if decode:
        y = cw[0:1] * s0_ref[...] + cw[1:2] * s1_ref[...] + cw[2:3] * z
        z_o[...] = z
    else:
        @pl.when(i % tiles_per_seq == 0)
        def _():
            zbuf[0:8, :] = jnp.zeros((8, BRANCH_W), F32)
        zbuf[8:8 + tm, :] = z
        y = cw[0:1] * zbuf[6:6 + tm, :] + cw[1:2] * zbuf[7:7 + tm, :] + cw[2:3] * z
        zbuf[0:8, :] = z[tm - 8:tm]
        conv_o[0] = z[tm - 8:tm]
    ob_o[...] = (sb * y).astype(act)
    du = proj(9)
    dv = proj(10)
    if decode:
        dv_o[...] = dv
        od_o[...] = du * (wd_ref[...] * dv + b0_ref[...])
    else:
        grp = lax.broadcasted_iota(jnp.int32, (CHUNK, BRANCH_W), 1) // (BRANCH_W // CH_GROUPS)
        for c0 in range(0, tm, CHUNK):
            dvc = dv[c0:c0 + CHUNK].astype(BF16)
            mixed = bsf_ref[...]
            for gi in range(CH_GROUPS):
                mixed = mixed + jnp.where(grp == gi, _dot(wsc_ref[gi], dvc), 0.0)
            od_o[c0:c0 + CHUNK, :] = (du[c0:c0 + CHUNK] * mixed).astype(BF16)


def _inproj_prompt(x, g, w_qkv, tabs, conv_w, wsc, bsf, n_seq, seq):
    n = x.shape[0]
    tm = min(512, seq)
    tps = seq // tm
    row = lambda i: (i, 0)
    const2 = lambda i: (0, 0)
    tab_spec = pl.BlockSpec((tm, BRANCH_W), lambda i: (i % tps, 0))
    out_spec = pl.BlockSpec((tm, BRANCH_W), row)
    f32o = jax.ShapeDtypeStruct((n, BRANCH_W), F32)
    bfo = jax.ShapeDtypeStruct((n, BRANCH_W), BF16)
    outs = pl.pallas_call(
        functools.partial(_inproj_kernel, tm=tm, tiles_per_seq=tps, decode=False),
        grid=(n // tm,),
        in_specs=[pl.BlockSpec((tm, D_MODEL), row), pl.BlockSpec((1, D_MODEL), const2),
                  pl.BlockSpec((D_MODEL, N_QKV_COLS), const2)] + [tab_spec] * 4 +
                 [pl.BlockSpec((3, BRANCH_W), const2),
                  pl.BlockSpec((CH_GROUPS, CHUNK, CHUNK), lambda i: (0, 0, 0)),
                  pl.BlockSpec((CHUNK, BRANCH_W), const2)],
        out_specs=[out_spec] * 4 + [pl.BlockSpec((1, 8, BRANCH_W), lambda i: (i // tps, 0, 0))] + [out_spec] * 8,
        out_shape=[f32o] * 4 + [jax.ShapeDtypeStruct((n_seq, 8, BRANCH_W), F32)] + [bfo] * 8,
        scratch_shapes=[pltpu.VMEM((tm + 8, BRANCH_W), F32)],
        compiler_params=_params(("arbitrary",)), name="inproj_prompt",
    )(x, g, w_qkv, *tabs, conv_w, wsc, bsf)
    return outs


def _inproj_decode(x, g, w_qkv, tabs, conv_w, wdiag, b0, s0, s1):
    n = x.shape[0]
    full = lambda shape: pl.BlockSpec(shape, lambda i: (0,) * len(shape))
    nb = full((n, BRANCH_W))
    f32o = jax.ShapeDtypeStruct((n, BRANCH_W), F32)
    return pl.pallas_call(
        functools.partial(_inproj_kernel, tm=n, tiles_per_seq=1, decode=True),
        grid=(1,),
        in_specs=[full((n, D_MODEL)), full((1, D_MODEL)), full((D_MODEL, N_QKV_COLS))] + [nb] * 4 +
                 [full((3, BRANCH_W)), full((1, BRANCH_W)), full((1, BRANCH_W)), nb, nb],
        out_specs=[nb] * 10,
        out_shape=[f32o] * 10,
        compiler_params=_params(("arbitrary",)), name="inproj_decode",
    )(x, g, w_qkv, *tabs, conv_w, wdiag, b0, s0, s1)


def _stack_heads(q_bf16, n_stack, width):
    q = q_bf16.astype(F32)
    grp = lax.broadcasted_iota(jnp.int32, q.shape, 1) // width
    return jnp.concatenate([jnp.where(grp == j, q, 0.0) for j in range(n_stack)], axis=0).astype(BF16)


def _pair_select(lo, hi):
    lane = lax.broadcasted_iota(jnp.int32, lo.shape, 1)
    return jnp.where(lane < MB_DH, lo, hi)


def _half_lane_mean_sq(o):
    lane = lax.broadcasted_iota(jnp.int32, o.shape, 1)
    sq = o * o
    lo = jnp.sum(jnp.where(lane < MB_DH, sq, 0.0), axis=1, keepdims=True)
    hi = jnp.sum(jnp.where(lane < MB_DH, 0.0, sq), axis=1, keepdims=True)
    return jnp.where(lane < MB_DH, lo, hi) * (1.0 / MB_DH)


def _subln(o_pair, g_pair, lam_init):
    return o_pair * lax.rsqrt(_half_lane_mean_sq(o_pair) + NORM_EPS) * g_pair * (1.0 - lam_init)


def _diff_attn_kernel(lam_ref, q_ref, k_ref, v_ref, g_ref, o_ref, qs_ref, m_ref, l_ref, acc_ref,
                      *, tq, tk, lam_init):
    i = pl.program_id(1)
    q_start = i * tq
    n_stack = 2 * DA_HEADS
    qs_ref[...] = _stack_heads(q_ref[...], n_stack, DA_DQK)
    m_ref[...] = jnp.full(m_ref.shape, NEG_INF, F32)
    l_ref[...] = jnp.zeros(l_ref.shape, F32)
    acc_ref[...] = jnp.zeros(acc_ref.shape, F32)

    def step(c, masked):
        k0 = pl.multiple_of(c * tk, tk)
        s = _dot_nt(qs_ref[...], k_ref[pl.ds(k0, tk), :])
        if masked:
            qpos = q_start + (lax.broadcasted_iota(jnp.int32, s.shape, 0) & (tq - 1))
            kpos = k0 + lax.broadcasted_iota(jnp.int32, s.shape, 1)
            s = jnp.where(kpos <= qpos, s, NEG_INF)
        m_old = m_ref[...]
        m_new = jnp.maximum(m_old, jnp.max(s, axis=1, keepdims=True))
        alpha = jnp.exp(m_old - m_new)
        p = jnp.exp(s - m_new)
        l_ref[...] = alpha * l_ref[...] + jnp.sum(p, axis=1, keepdims=True)
        m_ref[...] = m_new
        pb = p.astype(BF16)
        for h in range(DA_HEADS):
            rows = slice(2 * h * tq, (2 * h + 2) * tq)
            pv = _dot(pb[rows], v_ref[pl.ds(k0, tk), (h // 2) * LANES:(h // 2 + 1) * LANES])
            acc_ref[rows, :] = alpha[rows] * acc_ref[rows, :] + pv

    n_full = q_start // tk
    lax.fori_loop(0, n_full, lambda c, carry: (step(c, False), carry)[1], 0)
    step(n_full, True)

    lam = lam_ref[0]
    inv_l = 1.0 / l_ref[...]
    heads = []
    for h in range(DA_HEADS):
        r0, r1 = slice(2 * h * tq, (2 * h + 1) * tq), slice((2 * h + 1) * tq, (2 * h + 2) * tq)
        heads.append(acc_ref[r0, :] * inv_l[r0] - lam * (acc_ref[r1, :] * inv_l[r1]))
    g_pair = g_ref[...]
    o_ref[...] = jnp.concatenate(
        [_subln(_pair_select(heads[0], heads[1]), g_pair, lam_init),
         _subln(_pair_select(heads[2], heads[3]), g_pair, lam_init)], axis=1).astype(BF16)


def _diff_attn_prompt(lam, q, k, v, g_pair, n_seq, seq, lam_init):
    tq = min(128, seq)
    tk = min(512, seq)
    nq = seq // tq
    n_stack = 2 * DA_HEADS
    return pl.pallas_call(
        functools.partial(_diff_attn_kernel, tq=tq, tk=tk, lam_init=lam_init),
        grid=(n_seq, nq),
        in_specs=[pl.BlockSpec(memory_space=pltpu.SMEM),
                  pl.BlockSpec((tq, BRANCH_W), lambda b, i: (b * nq + i, 0)),
                  pl.BlockSpec((seq, BRANCH_W), lambda b, i: (b, 0)),
                  pl.BlockSpec((seq, BRANCH_W), lambda b, i: (b, 0)),
                  pl.BlockSpec((1, LANES), lambda b, i: (0, 0))],
        out_specs=pl.BlockSpec((tq, BRANCH_W), lambda b, i: (b * nq + i, 0)),
        out_shape=jax.ShapeDtypeStruct((n_seq * seq, BRANCH_W), BF16),
        scratch_shapes=[pltpu.VMEM((n_stack * tq, BRANCH_W), BF16),
                        pltpu.VMEM((n_stack * tq, 1), F32), pltpu.VMEM((n_stack * tq, 1), F32),
                        pltpu.VMEM((n_stack * tq, LANES), F32)],
        compiler_params=_params(("parallel", "arbitrary")), name="diff_attn_prompt",
    )(lam, q, k, v, g_pair)


def _top3_bits(gs, n_valid):
    lane = lax.broadcasted_iota(jnp.int32, gs.shape, 1)
    lane_f = lane.astype(F32)
    gs = jnp.where(lane < n_valid, gs, NEG_INF)
    bits = jnp.zeros((gs.shape[0], 1), jnp.int32)
    for _ in range(MOBA_TOPK):
        mx = jnp.max(gs, axis=1, keepdims=True)
        idx = jnp.min(jnp.where(gs == mx, lane_f, 1e9), axis=1, keepdims=True).astype(jnp.int32)
        bits = bits | jnp.where(idx < n_valid, jnp.left_shift(1, idx), 0)
        gs = jnp.where(lane == idx, -3e38, gs)
    return bits


def _moba_attn_kernel(q_ref, k_ref, v_ref, o_ref, qs_ref, km_ref, m_ref, l_ref, acc_ref, *, tq, n_blocks):
    i = pl.program_id(1)
    q_start = i * tq
    own = q_start // MOBA_BLOCK
    tk = MOBA_BLOCK

    @pl.when(i == 0)
    def _():
        km_ref[...] = jnp.zeros(km_ref.shape, F32)
        for j in range(n_blocks):
            blk = k_ref[j * MOBA_BLOCK:(j + 1) * MOBA_BLOCK, :].astype(F32)
            km_ref[j:j + 1, :] = jnp.mean(blk, axis=0, keepdims=True)

    qs_ref[...] = _stack_heads(q_ref[...], MB_HEADS, MB_DH)
    gs = lax.dot_general(qs_ref[...].astype(F32), km_ref[...], (((1,), (1,)), ((), ())),
                         precision=lax.Precision.HIGHEST, preferred_element_type=F32)
    bits = jnp.broadcast_to(_top3_bits(gs, own), (MB_HEADS * tq, LANES))

    def scores(c):
        k0 = pl.multiple_of(c * tk, tk)
        return k0, _dot_nt(qs_ref[...], k_ref[pl.ds(k0, tk), :])

    def accumulate(k0, s, m_old, first):
        m_new = jnp.maximum(m_old, jnp.max(s, axis=1, keepdims=True))
        p = jnp.exp(s - m_new)
        pb = p.astype(BF16)
        psum = jnp.sum(p, axis=1, keepdims=True)
        alpha = jnp.exp(m_old - m_new)
        l_ref[...] = psum if first else alpha * l_ref[...] + psum
        m_ref[...] = m_new
        for h in range(MB_HEADS):
            rows = slice(h * tq, (h + 1) * tq)
            pv = _dot(pb[rows], v_ref[pl.ds(k0, tk), (h // 2) * LANES:(h // 2 + 1) * LANES])
            acc_ref[rows, :] = pv if first else alpha[rows] * acc_ref[rows, :] + pv

    k0, s = scores(own)
    qpos = q_start + (lax.broadcasted_iota(jnp.int32, s.shape, 0) & (tq - 1))
    kpos = k0 + lax.broadcasted_iota(jnp.int32, s.shape, 1)
    accumulate(k0, jnp.where(kpos <= qpos, s, NEG_INF), jnp.full((MB_HEADS * tq, 1), NEG_INF, F32), True)

    def body(c, carry):
        k0, s = scores(c)
        sel = (jnp.right_shift(bits, c) & 1) == 1
        s = jnp.concatenate([jnp.where(sel, s[:, :LANES], NEG_INF), jnp.where(sel, s[:, LANES:], NEG_INF)],
                            axis=1)
        accumulate(k0, s, m_ref[...], False)
        return carry

    lax.fori_loop(0, own, body, 0)

    inv_l = 1.0 / l_ref[...]
    heads = [acc_ref[h * tq:(h + 1) * tq, :] * inv_l[h * tq:(h + 1) * tq] for h in range(MB_HEADS)]
    o_ref[...] = jnp.concatenate([_pair_select(heads[0], heads[1]), _pair_select(heads[2], heads[3])],
                                 axis=1).astype(BF16)


def _moba_attn_prompt(q, k, v, n_seq, seq):
    tq = min(128, seq)
    nq = seq // tq
    n_blocks = seq // MOBA_BLOCK
    assert n_blocks <= LANES and n_blocks <= 31
    return pl.pallas_call(
        functools.partial(_moba_attn_kernel, tq=tq, n_blocks=n_blocks),
        grid=(n_seq, nq),
        in_specs=[pl.BlockSpec((tq, BRANCH_W), lambda b, i: (b * nq + i, 0)),
                  pl.BlockSpec((seq, BRANCH_W), lambda b, i: (b, 0)),
                  pl.BlockSpec((seq, BRANCH_W), lambda b, i: (b, 0))],
        out_specs=pl.BlockSpec((tq, BRANCH_W), lambda b, i: (b * nq + i, 0)),
        out_shape=jax.ShapeDtypeStruct((n_seq * seq, BRANCH_W), BF16),
        scratch_shapes=[pltpu.VMEM((MB_HEADS * tq, BRANCH_W), BF16),
                        pltpu.VMEM((LANES, BRANCH_W), F32),
                        pltpu.VMEM((MB_HEADS * tq, 1), F32), pltpu.VMEM((MB_HEADS * tq, 1), F32),
                        pltpu.VMEM((MB_HEADS * tq, LANES), F32)],
        compiler_params=_params(("arbitrary", "arbitrary")), name="moba_attn_prompt",
    )(q, k, v)


PAGES_PER_STEP = 16


def _cache_pages(cache):
    depth, pool, page, n_heads, dh = cache.shape
    return jnp.transpose(cache, (0, 1, 3, 4, 2)).reshape(depth, pool, n_heads * dh, page)


def _page_specs(n_pages_step, layer):
    def spec(j):
        return pl.BlockSpec((1, 1, BRANCH_W, PAGE_SIZE),
                            lambda b, p, pt, *_: (layer, pt[b, p * n_pages_step + j], 0, 0))
    return [spec(j) for j in range(n_pages_step)]


def _hi_lo_rows(x):
    hi = x.astype(BF16).astype(F32)
    return jnp.concatenate([hi, x - hi], axis=0).astype(BF16)


def _fold_rows(y):
    return y[0:SUBLANES] + y[SUBLANES:2 * SUBLANES]


def _scores_split(q16, page):
    k_hi, k_lo = _split_bf16(page)
    return _fold_rows(_dot(q16, k_hi) + _dot(q16, k_lo))


def _values_split(p16, page):
    v_hi, v_lo = _split_bf16(page)
    return _fold_rows(_dot_nt(p16, v_hi) + _dot_nt(p16, v_lo))


def _diff_decode_kernel(pt_ref, lam_ref, q_ref, kn_ref, vn_ref, g_ref, *refs, pps, lam_init):
    k_refs, v_refs = refs[:pps], refs[pps:2 * pps]
    o_ref, qs_ref, m_ref, l_ref, acc_ref = refs[2 * pps:]
    p_idx = pl.program_id(1)
    n_stack = 2 * DA_HEADS

    @pl.when(p_idx == 0)
    def _():
        grp = lax.broadcasted_iota(jnp.int32, (n_stack, BRANCH_W), 1) // DA_DQK
        row = lax.broadcasted_iota(jnp.int32, (n_stack, BRANCH_W), 0)
        qs = jnp.where(grp == row, jnp.broadcast_to(q_ref[0], (n_stack, BRANCH_W)), 0.0)
        qs_ref[...] = _hi_lo_rows(qs)
        m_ref[...] = jnp.sum(qs * kn_ref[0], axis=1, keepdims=True)
        l_ref[...] = jnp.ones(l_ref.shape, F32)
        acc_ref[...] = jnp.broadcast_to(vn_ref[0], acc_ref.shape)

    q16 = qs_ref[...]
    s = jnp.concatenate([_scores_split(q16, k_refs[j][0, 0]) for j in range(pps)], axis=1)
    m_old = m_ref[...]
    m_new = jnp.maximum(m_old, jnp.max(s, axis=1, keepdims=True))
    alpha = jnp.exp(m_old - m_new)
    p = jnp.exp(s - m_new)
    l_ref[...] = alpha * l_ref[...] + jnp.sum(p, axis=1, keepdims=True)
    m_ref[...] = m_new
    p16 = _hi_lo_rows(p)
    pv = _values_split(p16[:, 0:PAGE_SIZE], v_refs[0][0, 0])
    for j in range(1, pps):
        pv = pv + _values_split(p16[:, j * PAGE_SIZE:(j + 1) * PAGE_SIZE], v_refs[j][0, 0])
    acc_ref[...] = alpha * acc_ref[...] + pv

    @pl.when(p_idx == pl.num_programs(1) - 1)
    def _():
        lam = lam_ref[0]
        o = acc_ref[...] / l_ref[...]
        grp = lax.broadcasted_iota(jnp.int32, (1, BRANCH_W), 1) // MB_DH
        out = jnp.zeros((1, BRANCH_W), F32)
        for h in range(DA_HEADS):
            out = out + jnp.where(grp == h, o[2 * h:2 * h + 1] - lam * o[2 * h + 1:2 * h + 2], 0.0)
        lo, hi = out[:, 0:LANES], out[:, LANES:2 * LANES]
        g_pair = g_ref[...]
        o_ref[0] = jnp.concatenate([_subln(lo, g_pair, lam_init), _subln(hi, g_pair, lam_init)], axis=1)


def _diff_attn_decode(page_table, lam, q, k_new, v_new, g_pair, pages_k, pages_v, layer, lam_init):
    n_seq, n_pages = page_table.shape
    pps = min(PAGES_PER_STEP, n_pages)
    row3 = pl.BlockSpec((1, 1, BRANCH_W), lambda b, p, pt: (b, 0, 0))
    n_stack = 2 * DA_HEADS
    out = pl.pallas_call(
        functools.partial(_diff_decode_kernel, pps=pps, lam_init=lam_init),
        grid_spec=pltpu.PrefetchScalarGridSpec(
            num_scalar_prefetch=1, grid=(n_seq, n_pages // pps),
            in_specs=[pl.BlockSpec(memory_space=pltpu.SMEM), row3, row3, row3,
                      pl.BlockSpec((1, LANES), lambda b, p, pt: (0, 0))] +
                     _page_specs(pps, layer) + _page_specs(pps, layer),
            out_specs=row3,
            scratch_shapes=[pltpu.VMEM((2 * n_stack, BRANCH_W), BF16), pltpu.VMEM((n_stack, 1), F32),
                            pltpu.VMEM((n_stack, 1), F32), pltpu.VMEM((n_stack, BRANCH_W), F32)]),
        out_shape=jax.ShapeDtypeStruct((n_seq, 1, BRANCH_W), F32),
        compiler_params=_params(("arbitrary", "arbitrary")), name="diff_attn_decode",
    )(page_table, lam, q.reshape(n_seq, 1, BRANCH_W), k_new.reshape(n_seq, 1, BRANCH_W),
      v_new.reshape(n_seq, 1, BRANCH_W), g_pair, *([pages_k] * pps), *([pages_v] * pps))
    return out.reshape(n_seq, BRANCH_W)


def _moba_gate_kernel(pt_ref, q_ref, *refs, pps):
    k_refs, o_ref = refs[:pps], refs[pps]
    ppb = MOBA_BLOCK // PAGE_SIZE
    q_col = q_ref[0]
    for j in range(pps // ppb):
        tot = k_refs[ppb * j][0, 0]
        for t in range(1, ppb):
            tot = tot + k_refs[ppb * j + t][0, 0]
        prod = tot * q_col
        rows = [jnp.sum(prod[h * MB_DH:(h + 1) * MB_DH], axis=0, keepdims=True) for h in range(MB_HEADS)]
        rows.append(jnp.zeros((SUBLANES - MB_HEADS, PAGE_SIZE), F32))
        o_ref[0, j] = jnp.concatenate(rows, axis=0) * (1.0 / MOBA_BLOCK)


def _moba_select_kernel(g_ref, idx_ref, *, n_blocks):
    g = jnp.sum(g_ref[0], axis=1, keepdims=True)
    r = lax.broadcasted_iota(jnp.int32, g.shape, 0)
    r_f = r.astype(F32)
    out_row = lax.broadcasted_iota(jnp.int32, (SUBLANES, LANES), 0)
    out_lane = lax.broadcasted_iota(jnp.int32, (SUBLANES, LANES), 1)
    out = jnp.zeros((SUBLANES, LANES), jnp.int32)
    for h in range(MB_HEADS):
        gh = jnp.where((r & (SUBLANES - 1)) == h, g, -3e38)
        for t in range(MOBA_TOPK):
            mx = jnp.max(gh, axis=0, keepdims=True)
            idx = jnp.min(jnp.where(gh == mx, r_f, 1e9), axis=0, keepdims=True).astype(jnp.int32)
            gh = jnp.where(r == idx, -3e38, gh)
            out = jnp.where((out_row == t) & (out_lane == h), jnp.right_shift(idx, 3), out)
    idx_ref[0] = out


def _moba_decode_kernel(pt_ref, top_ref, q_ref, kn_ref, vn_ref, k0_ref, k1_ref, v0_ref, v1_ref,
                        o_ref, m_ref, l_ref, acc_ref):
    h = pl.program_id(1)
    t = pl.program_id(2)
    grp = lax.broadcasted_iota(jnp.int32, (1, BRANCH_W), 1) // MB_DH
    qh = jnp.where(grp == h, q_ref[0], 0.0)

    @pl.when(t == 0)
    def _():
        m_ref[...] = jnp.sum(qh * kn_ref[0], axis=1, keepdims=True)
        l_ref[...] = jnp.ones(l_ref.shape, F32)
        acc_ref[...] = vn_ref[0]

    q16 = _hi_lo_rows(jnp.broadcast_to(qh, (SUBLANES, BRANCH_W)))
    s = jnp.concatenate([_scores_split(q16, k0_ref[0, 0]), _scores_split(q16, k1_ref[0, 0])],
                        axis=1)[0:1]
    m_old = m_ref[...]
    m_new = jnp.maximum(m_old, jnp.max(s, axis=1, keepdims=True))
    alpha = jnp.exp(m_old - m_new)
    p = jnp.exp(s - m_new)
    l_ref[...] = alpha * l_ref[...] + jnp.sum(p, axis=1, keepdims=True)
    m_ref[...] = m_new
    p16 = _hi_lo_rows(jnp.broadcast_to(p, (SUBLANES, MOBA_BLOCK)))
    pv = (_values_split(p16[:, 0:PAGE_SIZE], v0_ref[0, 0]) +
          _values_split(p16[:, PAGE_SIZE:], v1_ref[0, 0]))[0:1]
    acc_ref[...] = alpha * acc_ref[...] + pv

    @pl.when(t == MOBA_TOPK - 1)
    def _():
        o_ref[0, 0] = acc_ref[...] / l_ref[...]


def _moba_attn_decode(page_table, q, k_new, v_new, pages_k, pages_v, layer):
    n_seq, n_pages = page_table.shape
    pps = min(PAGES_PER_STEP, n_pages)
    ppb = MOBA_BLOCK // PAGE_SIZE
    n_blocks = n_pages // ppb
    assert n_blocks >= MOBA_TOPK and pps % ppb == 0 and ppb == 2
    q3 = q.reshape(n_seq, 1, BRANCH_W)
    gate = pl.pallas_call(
        functools.partial(_moba_gate_kernel, pps=pps),
        grid_spec=pltpu.PrefetchScalarGridSpec(
            num_scalar_prefetch=1, grid=(n_seq, n_pages // pps),
            in_specs=[pl.BlockSpec((1, BRANCH_W, 1), lambda b, p, pt: (b, 0, 0))] + _page_specs(pps, layer),
            out_specs=pl.BlockSpec((1, pps // ppb, SUBLANES, PAGE_SIZE), lambda b, p, pt: (b, p, 0, 0))),
        out_shape=jax.ShapeDtypeStruct((n_seq, n_blocks, SUBLANES, PAGE_SIZE), F32),
        compiler_params=_params(("arbitrary", "arbitrary")), name="moba_gate_decode",
    )(page_table, q.reshape(n_seq, BRANCH_W, 1), *([pages_k] * pps))
    top = pl.pallas_call(
        functools.partial(_moba_select_kernel, n_blocks=n_blocks),
        grid=(n_seq,),
        in_specs=[pl.BlockSpec((1, n_blocks * SUBLANES, PAGE_SIZE), lambda b: (b, 0, 0))],
        out_specs=pl.BlockSpec((1, SUBLANES, LANES), lambda b: (b, 0, 0)),
        out_shape=jax.ShapeDtypeStruct((n_seq, SUBLANES, LANES), jnp.int32),
        compiler_params=_params(("parallel",)), name="moba_select_decode",
    )(gate.reshape(n_seq, n_blocks * SUBLANES, PAGE_SIZE))
    top = top[:, :MOBA_TOPK, :MB_HEADS].transpose(0, 2, 1).reshape(-1)
    row3 = pl.BlockSpec((1, 1, BRANCH_W), lambda b, h, t, pt, tp: (b, 0, 0))

    def page(half):
        return pl.BlockSpec(
            (1, 1, BRANCH_W, PAGE_SIZE),
            lambda b, h, t, pt, tp: (layer, pt[b, tp[(b * MB_HEADS + h) * MOBA_TOPK + t] * ppb + half], 0, 0))

    o = pl.pallas_call(
        _moba_decode_kernel,
        grid_spec=pltpu.PrefetchScalarGridSpec(
            num_scalar_prefetch=2, grid=(n_seq, MB_HEADS, MOBA_TOPK),
            in_specs=[row3, row3, row3, page(0), page(1), page(0), page(1)],
            out_specs=pl.BlockSpec((1, 1, 1, BRANCH_W), lambda b, h, t, pt, tp: (b, h, 0, 0)),
            scratch_shapes=[pltpu.VMEM((1, 1), F32), pltpu.VMEM((1, 1), F32),
                            pltpu.VMEM((1, BRANCH_W), F32)]),
        out_shape=jax.ShapeDtypeStruct((n_seq, MB_HEADS, 1, BRANCH_W), F32),
        compiler_params=_params(("arbitrary", "arbitrary", "arbitrary")), name="moba_attn_decode",
    )(page_table, top, q3, k_new.reshape(n_seq, 1, BRANCH_W), v_new.reshape(n_seq, 1, BRANCH_W),
      pages_k, pages_k, pages_v, pages_v)
    o = o.reshape(n_seq, MB_HEADS, MB_HEADS, MB_DH)
    return jnp.concatenate([o[:, h, h] for h in range(MB_HEADS)], axis=-1)


def _merge_kernel(x_ref, g_ref, oa_ref, ob_ref, oc_ref, od_ref, wgl_ref, wbr_ref, wout_ref, xo_ref):
    x = x_ref[...]
    hb = _rms(x, g_ref[...]).astype(BF16)
    merged = None
    for k, br in enumerate((oa_ref, ob_ref, oc_ref, od_ref)):
        gate = jax.nn.sigmoid(_dot(hb, wgl_ref[:, k * D_MODEL:(k + 1) * D_MODEL]))
        term = gate * _dot(br[...], wbr_ref[k])
        merged = term if merged is None else merged + term
    xo_ref[...] = x + _dot(merged.astype(BF16), wout_ref[...])


def _merge(x, g, oa, ob, oc, od, w_gl, w_br, w_out):
    n = x.shape[0]
    tm = min(512, n)
    row = lambda i: (i, 0)
    const2 = lambda i: (0, 0)
    br_spec = pl.BlockSpec((tm, BRANCH_W), row)
    return pl.pallas_call(
        _merge_kernel, grid=(n // tm,),
        in_specs=[pl.BlockSpec((tm, D_MODEL), row), pl.BlockSpec((1, D_MODEL), const2)] + [br_spec] * 4 +
                 [pl.BlockSpec((D_MODEL, N_BRANCH * D_MODEL), const2),
                  pl.BlockSpec((N_BRANCH, BRANCH_W, D_MODEL), lambda i: (0, 0, 0)),
                  pl.BlockSpec((D_MODEL, D_MODEL), const2)],
        out_specs=pl.BlockSpec((tm, D_MODEL), row),
        out_shape=jax.ShapeDtypeStruct((n, D_MODEL), F32),
        compiler_params=_params(("parallel",)), name="merge",
    )(x, g, oa, ob, oc, od, w_gl, w_br, w_out)


def _merge_decode_kernel(x_ref, g_ref, br_ref, wgl_ref, wbr_ref, wout_ref, xo_ref, acc_ref):
    k = pl.program_id(0)
    x = x_ref[...]
    h = _rms(x, g_ref[...])
    term = jax.nn.sigmoid(_dot_f32(h, wgl_ref[...])) * _dot_f32(br_ref[0], wbr_ref[0])

    @pl.when(k == 0)
    def _():
        acc_ref[...] = term

    @pl.when(k > 0)
    def _():
        acc_ref[...] += term

    @pl.when(k == N_BRANCH - 1)
    def _():
        xo_ref[...] = x + _dot_f32(acc_ref[...], wout_ref[...])


def _merge_decode(x, g, branches, w_gl, w_br, w_out):
    n = x.shape[0]
    const2 = lambda k: (0, 0)
    return pl.pallas_call(
        _merge_decode_kernel, grid=(N_BRANCH,),
        in_specs=[pl.BlockSpec((n, D_MODEL), const2), pl.BlockSpec((1, D_MODEL), const2),
                  pl.BlockSpec((1, n, BRANCH_W), lambda k: (k, 0, 0)),
                  pl.BlockSpec((D_MODEL, D_MODEL), lambda k: (0, k)),
                  pl.BlockSpec((1, BRANCH_W, D_MODEL), lambda k: (k, 0, 0)),
                  pl.BlockSpec((D_MODEL, D_MODEL), const2)],
        out_specs=pl.BlockSpec((n, D_MODEL), const2),
        out_shape=jax.ShapeDtypeStruct((n, D_MODEL), F32),
        scratch_shapes=[pltpu.VMEM((n, D_MODEL), F32)],
        compiler_params=_params(("arbitrary",)), name="merge_decode",
    )(x, g, branches, w_gl, w_br, w_out)


def _router_kernel(x_ref, g_ref, w_ref, b_ref, h_ref, comb_ref):
    h = _rms(x_ref[...], g_ref[...])
    h_ref[...] = h.astype(h_ref.dtype)
    lg = _dot_f32(h, w_ref[...]) + b_ref[...]
    lane = lax.broadcasted_iota(jnp.int32, lg.shape, 1)
    lane_f = lane.astype(F32)
    is_g = (lane >= N_EXPERTS) & (lane < N_EXPERTS + N_EGROUPS)
    mxg = jnp.max(jnp.where(is_g, lg, NEG_INF), axis=1, keepdims=True)
    gsel = jnp.min(jnp.where(is_g & (lg == mxg), lane_f, 1e9), axis=1, keepdims=True).astype(jnp.int32)
    gp = 1.0 / jnp.sum(jnp.where(is_g, jnp.exp(lg - mxg), 0.0), axis=1, keepdims=True)
    e_lo = (gsel - N_EXPERTS) * E_PER_GROUP
    in_g = (lane >= e_lo) & (lane < e_lo + E_PER_GROUP)
    v1 = jnp.max(jnp.where(in_g, lg, NEG_INF), axis=1, keepdims=True)
    i1 = jnp.min(jnp.where(in_g & (lg == v1), lane_f, 1e9), axis=1, keepdims=True).astype(jnp.int32)
    rest = in_g & (lane != i1)
    v2 = jnp.max(jnp.where(rest, lg, NEG_INF), axis=1, keepdims=True)
    i2 = jnp.min(jnp.where(rest & (lg == v2), lane_f, 1e9), axis=1, keepdims=True).astype(jnp.int32)
    e2 = jnp.exp(v2 - v1)
    w1 = gp / (1.0 + e2)
    w2 = gp * e2 / (1.0 + e2)
    comb_ref[...] = jnp.where(lane == i1, w1, 0.0) + jnp.where(lane == i2, w2, 0.0)


def _router(x, g, w_router, b_router, h_dtype):
    n = x.shape[0]
    tm = min(512, n)
    row = lambda i: (i, 0)
    const2 = lambda i: (0, 0)
    return pl.pallas_call(
        _router_kernel, grid=(n // tm,),
        in_specs=[pl.BlockSpec((tm, D_MODEL), row), pl.BlockSpec((1, D_MODEL), const2),
                  pl.BlockSpec((D_MODEL, LANES), const2), pl.BlockSpec((1, LANES), const2)],
        out_specs=[pl.BlockSpec((tm, D_MODEL), row), pl.BlockSpec((tm, LANES), row)],
        out_shape=[jax.ShapeDtypeStruct((n, D_MODEL), h_dtype), jax.ShapeDtypeStruct((n, LANES), F32)],
        compiler_params=_params(("parallel",)), name="router",
    )(x, g, w_router, b_router)


def _experts_kernel(x_ref, h_ref, comb_ref, wg_ref, wu_ref, wd_ref, gf_ref, xo_ref, acc_ref,
                    *, final_norm, precise):
    e = pl.program_id(1)
    dot = _dot_f32 if precise else _dot

    @pl.when(e == 0)
    def _():
        acc_ref[...] = x_ref[...]

    comb = comb_ref[...]
    lane = lax.broadcasted_iota(jnp.int32, comb.shape, 1)
    cw = jnp.sum(jnp.where(lane == e, comb, 0.0), axis=1, keepdims=True)
    hb = h_ref[...]
    a = jax.nn.silu(dot(hb, wg_ref[0])) * dot(hb, wu_ref[0])
    acc_ref[...] += dot((a * cw).astype(hb.dtype), wd_ref[0])

    @pl.when(e == pl.num_programs(1) - 1)
    def _():
        xo = acc_ref[...]
        xo_ref[...] = _rms(xo, gf_ref[...]) if final_norm else xo


def _experts(x, h, comb, w_gate, w_up, w_down, g_final, final_norm, precise):
    n = x.shape[0]
    tm = min(1024, n)
    row = lambda i, e: (i, 0)
    return pl.pallas_call(
        functools.partial(_experts_kernel, final_norm=final_norm, precise=precise),
        grid=(n // tm, N_EXPERTS),
        in_specs=[pl.BlockSpec((tm, D_MODEL), row), pl.BlockSpec((tm, D_MODEL), row),
                  pl.BlockSpec((tm, LANES), row),
                  pl.BlockSpec((1, D_MODEL, D_EXPERT), lambda i, e: (e, 0, 0)),
                  pl.BlockSpec((1, D_MODEL, D_EXPERT), lambda i, e: (e, 0, 0)),
                  pl.BlockSpec((1, D_EXPERT, D_MODEL), lambda i, e: (e, 0, 0)),
                  pl.BlockSpec((1, D_MODEL), lambda i, e: (0, 0))],
        out_specs=pl.BlockSpec((tm, D_MODEL), row),
        out_shape=jax.ShapeDtypeStruct((n, D_MODEL), F32),
        scratch_shapes=[pltpu.VMEM((tm, D_MODEL), F32)],
        compiler_params=_params(("parallel", "arbitrary")), name="experts",
    )(x, h, comb, w_gate, w_up, w_down, g_final)


def _layer_weights(l, W):
    w_in = W["w_in"][l]
    lam_init = 0.8 - 0.6 * math.exp(-0.3 * l)
    lam = (jnp.exp(jnp.sum(W["lambda_q1"][l] * W["lambda_k1"][l]))
           - jnp.exp(jnp.sum(W["lambda_q2"][l] * W["lambda_k2"][l])) + lam_init).reshape(1).astype(F32)
    tril = jnp.tril(jnp.ones((CHUNK, CHUNK), F32))
    ws, bs = W["chunk_ws"][l], W["chunk_b"][l]
    rep = BRANCH_W // CH_GROUPS
    router_w = jnp.concatenate(
        [W["router_e_w"][l].transpose(1, 0, 2).reshape(D_MODEL, N_EXPERTS), W["router_g_w"][l],
         jnp.zeros((D_MODEL, LANES - N_EXPERTS - N_EGROUPS), F32)], axis=1)
    router_b = jnp.concatenate(
        [W["router_e_b"][l].reshape(N_EXPERTS), W["router_g_b"][l],
         jnp.zeros((LANES - N_EXPERTS - N_EGROUPS,), F32)]).reshape(1, LANES)
    f32w = dict(
        w_qkv=w_in[:, :N_QKV_COLS], w_gl=w_in[:, N_QKV_COLS:], w_br=W["w_branch"][l], w_out=W["w_out"][l],
        w_gate=W["w_gate_e"][l].reshape(N_EXPERTS, D_MODEL, D_EXPERT),
        w_up=W["w_up_e"][l].reshape(N_EXPERTS, D_MODEL, D_EXPERT),
        w_down=W["w_down_e"][l].reshape(N_EXPERTS, D_EXPERT, D_MODEL))
    return dict(
        lam_init=lam_init, lam=lam,
        g_mix=W["norm_mix_g"][l].reshape(1, D_MODEL), g_ffn=W["norm_ffn_g"][l].reshape(1, D_MODEL),
        conv_w=W["conv_w"][l],
        wsc=(ws * tril).astype(BF16), bsf=jnp.repeat(bs.T, rep, axis=1),
        wdiag=jnp.repeat(ws[:, 0, 0], rep).reshape(1, BRANCH_W), b0=jnp.repeat(bs[:, 0], rep).reshape(1, BRANCH_W),
        g_sub=jnp.tile(W["subln_g"][l], LANES // MB_DH).reshape(1, LANES),
        router_w=router_w, router_b=router_b,
        f32=f32w, bf16={k: v.astype(BF16) for k, v in f32w.items()},
    )


def _ffn(x, lw, g_final, final_norm, precise):
    w = lw["f32"] if precise else lw["bf16"]
    h, comb = _router(x, lw["g_ffn"], lw["router_w"], lw["router_b"], F32 if precise else BF16)
    return _experts(x, h, comb, w["w_gate"], w["w_up"], w["w_down"], g_final, final_norm, precise)


def _run_prompt(x_prompt, LW, g_final):
    n_seq, seq, _ = x_prompt.shape
    x = x_prompt.reshape(n_seq * seq, D_MODEL)
    tabs = _rope_tables(seq, 0, 1)
    states = []
    for l, lw in enumerate(LW):
        w = lw["bf16"]
        (ak, av, mk, mv, conv, qa, ka, va, qm, km, vm, ob, od) = _inproj_prompt(
            x, lw["g_mix"], w["w_qkv"], tabs, lw["conv_w"], lw["wsc"], lw["bsf"], n_seq, seq)
        oa = _diff_attn_prompt(lw["lam"], qa, ka, va, lw["g_sub"], n_seq, seq, lw["lam_init"])
        oc = _moba_attn_prompt(qm, km, vm, n_seq, seq)
        x = _merge(x, lw["g_mix"], oa, ob, oc, od, w["w_gl"], w["w_br"], w["w_out"])
        x = _ffn(x, lw, g_final, l == len(LW) - 1, False)
        shp = (n_seq, seq, DA_HEADS, 2 * DA_DQK)
        states.append((ak.reshape(shp), av.reshape(shp), mk.reshape(shp), mv.reshape(shp), conv[:, 6:8]))
    return x.reshape(n_seq, seq, D_MODEL), states


def _run_sample(x_sample, caches, state_conv, page_table, LW, g_final):
    n_seq = x_sample.shape[0]
    past_len = page_table.shape[1] * PAGE_SIZE
    x = x_sample.reshape(n_seq, D_MODEL)
    tabs = _rope_tables(n_seq, past_len, 0)
    pk_d, pv_d, pk_m, pv_m = [_cache_pages(c) for c in caches]
    states = []
    for l, lw in enumerate(LW):
        w = lw["f32"]
        s0, s1 = state_conv[l][:, 0], state_conv[l][:, 1]
        (ak, av, mk, mv, dv, z, qa, qm, ob, od) = _inproj_decode(
            x, lw["g_mix"], w["w_qkv"], tabs, lw["conv_w"], lw["wdiag"], lw["b0"], s0, s1)
        oa = _diff_attn_decode(page_table, lw["lam"], qa, ak, av, lw["g_sub"], pk_d, pv_d, l, lw["lam_init"])
        oc = _moba_attn_decode(page_table, qm, mk, mv, pk_m, pv_m, l)
        x = _merge_decode(x, lw["g_mix"], jnp.stack([oa, ob, oc, od]), w["w_gl"], w["w_br"], w["w_out"])
        x = _ffn(x, lw, g_final, l == len(LW) - 1, True)
        shp = (n_seq, 1, DA_HEADS, 2 * DA_DQK)
        states.append((ak.reshape(shp), av.reshape(shp), mk.reshape(shp), mv.reshape(shp),
                       jnp.stack([s1, z], axis=1), dv.reshape(n_seq, 1, BRANCH_W)))
    return x.reshape(n_seq, 1, D_MODEL), states


def kernel(x_prompt, x_sample, cache_k_diff, cache_v_diff, cache_k_moba, cache_v_moba, state_conv, page_table, norm_mix_g, w_in, lambda_q1, lambda_k1, lambda_q2, lambda_k2, subln_g, conv_w, chunk_ws, chunk_b, w_branch, w_out, norm_ffn_g, router_g_w, router_g_b, router_e_w, router_e_b, w_gate_e, w_up_e, w_down_e, norm_final_g):
    W = dict(norm_mix_g=norm_mix_g, w_in=w_in, lambda_q1=lambda_q1, lambda_k1=lambda_k1,
             lambda_q2=lambda_q2, lambda_k2=lambda_k2, subln_g=subln_g, conv_w=conv_w,
             chunk_ws=chunk_ws, chunk_b=chunk_b, w_branch=w_branch, w_out=w_out, norm_ffn_g=norm_ffn_g,
             router_g_w=router_g_w, router_g_b=router_g_b, router_e_w=router_e_w, router_e_b=router_e_b,
             w_gate_e=w_gate_e, w_up_e=w_up_e, w_down_e=w_down_e)
    depth = w_in.shape[0]
    LW = [_layer_weights(l, W) for l in range(depth)]
    g_final = norm_final_g.reshape(1, D_MODEL)
    y_prompt, sp = _run_prompt(x_prompt, LW, g_final)
    y_sample, ss = _run_sample(x_sample, (cache_k_diff, cache_v_diff, cache_k_moba, cache_v_moba),
                               state_conv, page_table, LW, g_final)
    stack = lambda st, k: jnp.stack([s[k] for s in st])
    return (y_prompt, y_sample,
            stack(sp, 0), stack(sp, 1), stack(sp, 2), stack(sp, 3), stack(sp, 4),
            stack(ss, 0), stack(ss, 1), stack(ss, 2), stack(ss, 3), stack(ss, 4), stack(ss, 5))
```

```python
import functools
import math

import jax
import jax.numpy as jnp
from jax import lax
from jax.experimental import pallas as pl
from jax.experimental.pallas import tpu as pltpu

F32 = jnp.float32
BF16 = jnp.bfloat16

D_MODEL = 1024
BRANCH_W = 256
N_BRANCH = 4
DA_HEADS, DA_DQK = 4, 32
MB_HEADS, MB_DH = 4, 64
MOBA_BLOCK = 256
MOBA_TOPK = 3
CHUNK = 128
CH_GROUPS = 4
N_EGROUPS, E_PER_GROUP = 4, 8
N_EXPERTS = N_EGROUPS * E_PER_GROUP
D_EXPERT = 256
PAGE_SIZE = 128
ROPE_THETA = 10000.0
NORM_EPS = 1e-6
NEG_INF = -1e30
N_QKV_COLS = 11 * BRANCH_W
LOG2_E = math.log2(math.e)
DIFF_TQ, DIFF_TK = 128, 512
MOBA_TQ = 256

LANES = 128
SUBLANES = 8
VMEM_LIMIT = 48 * 1024 * 1024


def _params(sem):
    return pltpu.CompilerParams(dimension_semantics=sem, vmem_limit_bytes=VMEM_LIMIT)


def _rms(x, g):
    return x * lax.rsqrt(jnp.mean(x * x, axis=-1, keepdims=True) + NORM_EPS) * g


def _dot(a, b):
    return jnp.dot(a, b, preferred_element_type=F32)


def _dot_nt(a, b):
    return lax.dot_general(a, b, (((1,), (1,)), ((), ())), preferred_element_type=F32)


def _dot_f32(a, b):
    return jnp.dot(a, b, precision=lax.Precision.HIGHEST, preferred_element_type=F32)


def _split_bf16(x):
    hi = x.astype(BF16)
    return hi, (x - hi.astype(F32)).astype(BF16)


def _rope_table_kernel(ca_ref, sa_ref, cm_ref, sm_ref, *, pos0, step, rows):
    i = pl.program_id(0)
    r = lax.broadcasted_iota(jnp.int32, (rows, BRANCH_W), 0)
    c = lax.broadcasted_iota(jnp.int32, (rows, BRANCH_W), 1)
    pos = (pos0 + (i * rows + r) * step).astype(F32)

    def tables(d):
        half = d // 2
        ci = c & (d - 1)
        fi = (ci & (half - 1)).astype(F32)
        inv = jnp.exp((-math.log(ROPE_THETA) * fi) * (2.0 / d))
        ang = pos * inv
        sign = jnp.where(ci < half, -1.0, 1.0).astype(F32)
        return jnp.cos(ang), jnp.sin(ang) * sign

    ca_ref[...], sa_ref[...] = tables(DA_DQK)
    cm_ref[...], sm_ref[...] = tables(MB_DH)


def _rope_tables(n_rows, pos0, step):
    rows = min(n_rows, 512)
    spec = pl.BlockSpec((rows, BRANCH_W), lambda i: (i, 0))
    shp = jax.ShapeDtypeStruct((n_rows, BRANCH_W), F32)
    return pl.pallas_call(
        functools.partial(_rope_table_kernel, pos0=pos0, step=step, rows=rows),
        grid=(n_rows // rows,), out_specs=[spec] * 4, out_shape=[shp] * 4,
        compiler_params=_params(("parallel",)), name="rope_tables")()


def _rope(x, cos, sin_signed, d):
    half = d // 2
    c = lax.broadcasted_iota(jnp.int32, x.shape, 1)
    first = (c & (d - 1)) < half
    rot = jnp.where(first, pltpu.roll(x, BRANCH_W - half, 1), pltpu.roll(x, half, 1))
    return x * cos + rot * sin_signed


def _inproj_kernel(*refs, tm, tiles_per_seq, decode):
    if decode:
        (x_ref, g_ref, w_ref, ca_ref, sa_ref, cm_ref, sm_ref, cw_ref, wd_ref, b0_ref, s0_ref, s1_ref,
         ak_o, av_o, mk_o, mv_o, dv_o, z_o, qa_o, qm_o, ob_o, od_o) = refs
    else:
        (x_ref, g_ref, w_ref, ca_ref, sa_ref, cm_ref, sm_ref, cw_ref, wsc_ref, bsf_ref,
         ak_o, av_o, mk_o, mv_o, conv_o, qa_o, ka_o, va_o, qm_o, km_o, vm_o, ob_o, od_o, zbuf) = refs
    i = pl.program_id(0)
    act = F32 if decode else BF16
    hb = _rms(x_ref[...], g_ref[...]).astype(act)

    def proj(k):
        w = w_ref[:, k * BRANCH_W:(k + 1) * BRANCH_W]
        return _dot_f32(hb, w) if decode else _dot(hb, w)

    ca, sa, cm, sm = ca_ref[...], sa_ref[...], cm_ref[...], sm_ref[...]
    aq = _rope(proj(0), ca, sa, DA_DQK)
    ak = _rope(proj(1), ca, sa, DA_DQK)
    av = proj(2)
    ak_o[...] = ak
    av_o[...] = av
    exp_base = 1.0 if decode else LOG2_E
    qa_o[...] = (aq * (DA_DQK ** -0.5 * exp_base)).astype(act)
    mq = _rope(proj(6), cm, sm, MB_DH)
    mk = _rope(proj(7), cm, sm, MB_DH)
    mv = proj(8)
    mk_o[...] = mk
    mv_o[...] = mv
    qm_o[...] = (mq * (MB_DH ** -0.5 * exp_base)).astype(act)
    if not decode:
        ka_o[...] = ak.astype(BF16)
        km_o[...] = mk.astype(BF16)
        for c in range(tm // DIFF_TK):
            va_o[c] = av[c * DIFF_TK:(c + 1) * DIFF_TK].T.astype(BF16)
        for c in range(tm // MOBA_BLOCK):
            vm_o[c] = mv[c * MOBA_BLOCK:(c + 1) * MOBA_BLOCK].T.astype(BF16)
    sb = proj(3)
    z = proj(4) * proj(5)
    cw = cw_ref[...]
    if decode:
        y = cw[0:1] * s0_ref[...] + cw[1:2] * s1_ref[...] + cw[2:3] * z
        z_o[...] = z
    else:
        @pl.when(i % tiles_per_seq == 0)
        def _():
            zbuf[0:8, :] = jnp.zeros((8, BRANCH_W), F32)
        zbuf[8:8 + tm, :] = z
        y = cw[0:1] * zbuf[6:6 + tm, :] + cw[1:2] * zbuf[7:7 + tm, :] + cw[2:3] * z
        zbuf[0:8, :] = z[tm - 8:tm]
        conv_o[0] = z[tm - 8:tm]
    ob_o[...] = (sb * y).astype(act)
    du = proj(9)
    dv = proj(10)
    if decode:
        dv_o[...] = dv
        od_o[...] = du * (wd_ref[...] * dv + b0_ref[...])
    else:
        grp = lax.broadcasted_iota(jnp.int32, (CHUNK, BRANCH_W), 1) // (BRANCH_W // CH_GROUPS)
        for c0 in range(0, tm, CHUNK):
            dvc = dv[c0:c0 + CHUNK].astype(BF16)
            mixed = bsf_ref[...]
            for gi in range(CH_GROUPS):
                mixed = mixed + jnp.where(grp == gi, _dot(wsc_ref[gi], dvc), 0.0)
            od_o[c0:c0 + CHUNK, :] = (du[c0:c0 + CHUNK] * mixed).astype(BF16)


def _inproj_prompt(x, g, w_qkv, tabs, conv_w, wsc, bsf, n_seq, seq):
    n = x.shape[0]
    tm = min(512, seq)
    tps = seq // tm
    row = lambda i: (i, 0)
    const2 = lambda i: (0, 0)
    tab_spec = pl.BlockSpec((tm, BRANCH_W), lambda i: (i % tps, 0))
    out_spec = pl.BlockSpec((tm, BRANCH_W), row)
    f32o = jax.ShapeDtypeStruct((n, BRANCH_W), F32)
    bfo = jax.ShapeDtypeStruct((n, BRANCH_W), BF16)
    vt_spec = lambda chunk: pl.BlockSpec((tm // chunk, BRANCH_W, chunk), lambda i: (i, 0, 0))
    vt_shape = lambda chunk: jax.ShapeDtypeStruct((n // chunk, BRANCH_W, chunk), BF16)
    assert tm % DIFF_TK == 0 and tm % MOBA_BLOCK == 0
    outs = pl.pallas_call(
        functools.partial(_inproj_kernel, tm=tm, tiles_per_seq=tps, decode=False),
        grid=(n // tm,),
        in_specs=[pl.BlockSpec((tm, D_MODEL), row), pl.BlockSpec((1, D_MODEL), const2),
                  pl.BlockSpec((D_MODEL, N_QKV_COLS), const2)] + [tab_spec] * 4 +
                 [pl.BlockSpec((3, BRANCH_W), const2),
                  pl.BlockSpec((CH_GROUPS, CHUNK, CHUNK), lambda i: (0, 0, 0)),
                  pl.BlockSpec((CHUNK, BRANCH_W), const2)],
        out_specs=[out_spec] * 4 + [pl.BlockSpec((1, 8, BRANCH_W), lambda i: (i // tps, 0, 0))] +
                  [out_spec, out_spec, vt_spec(DIFF_TK), out_spec, out_spec, vt_spec(MOBA_BLOCK), out_spec, out_spec],
        out_shape=[f32o] * 4 + [jax.ShapeDtypeStruct((n_seq, 8, BRANCH_W), F32)] +
                  [bfo, bfo, vt_shape(DIFF_TK), bfo, bfo, vt_shape(MOBA_BLOCK), bfo, bfo],
        scratch_shapes=[pltpu.VMEM((tm + 8, BRANCH_W), F32)],
        compiler_params=_params(("arbitrary",)), name="inproj_prompt",
    )(x, g, w_qkv, *tabs, conv_w, wsc, bsf)
    return outs


def _inproj_decode(x, g, w_qkv, tabs, conv_w, wdiag, b0, s0, s1):
    n = x.shape[0]
    full = lambda shape: pl.BlockSpec(shape, lambda i: (0,) * len(shape))
    nb = full((n, BRANCH_W))
    f32o = jax.ShapeDtypeStruct((n, BRANCH_W), F32)
    return pl.pallas_call(
        functools.partial(_inproj_kernel, tm=n, tiles_per_seq=1, decode=True),
        grid=(1,),
        in_specs=[full((n, D_MODEL)), full((1, D_MODEL)), full((D_MODEL, N_QKV_COLS))] + [nb] * 4 +
                 [full((3, BRANCH_W)), full((1, BRANCH_W)), full((1, BRANCH_W)), nb, nb],
        out_specs=[nb] * 10,
        out_shape=[f32o] * 10,
        compiler_params=_params(("arbitrary",)), name="inproj_decode",
    )(x, g, w_qkv, *tabs, conv_w, wdiag, b0, s0, s1)


def _stack_heads_t(q_bf16, n_stack, width):
    qt = q_bf16.astype(F32).T
    grp = lax.broadcasted_iota(jnp.int32, qt.shape, 0) // width
    return jnp.concatenate([jnp.where(grp == j, qt, 0.0) for j in range(n_stack)], axis=1)


def _softmax_step(s, m_ref, l_ref):
    m_old = m_ref[...]
    m_new = jnp.maximum(m_old, jnp.max(s, axis=0, keepdims=True))
    alpha = jnp.exp2(m_old - m_new)
    p = jnp.exp2(s - m_new)
    l_ref[...] = alpha * l_ref[...] + jnp.sum(p, axis=0, keepdims=True)
    m_ref[...] = m_new
    return alpha, p.astype(BF16)


def _rows_subln(x, g, lam_init):
    ms = jnp.mean(x * x, axis=0, keepdims=True)
    return x * lax.rsqrt(ms + NORM_EPS) * g * (1.0 - lam_init)


def _half_lane_mean_sq(o):
    lane = lax.broadcasted_iota(jnp.int32, o.shape, 1)
    sq = o * o
    lo = jnp.sum(jnp.where(lane < MB_DH, sq, 0.0), axis=1, keepdims=True)
    hi = jnp.sum(jnp.where(lane < MB_DH, 0.0, sq), axis=1, keepdims=True)
    return jnp.where(lane < MB_DH, lo, hi) * (1.0 / MB_DH)


def _subln(o_pair, g_pair, lam_init):
    return o_pair * lax.rsqrt(_half_lane_mean_sq(o_pair) + NORM_EPS) * g_pair * (1.0 - lam_init)


def _diff_attn_kernel(lam_ref, q_ref, k_ref, vt_ref, g_ref, o_ref, qs_ref, m_ref, l_ref, acc_ref,
                      *, tq, tk, lam_init):
    i = pl.program_id(1)
    q_start = i * tq
    qs_ref[...] = _stack_heads_t(q_ref[...], 2 * DA_HEADS, DA_DQK).astype(BF16)
    m_ref[...] = jnp.full(m_ref.shape, NEG_INF, F32)
    l_ref[...] = jnp.zeros(l_ref.shape, F32)
    acc_ref[...] = jnp.zeros(acc_ref.shape, F32)

    def step(c, masked):
        k0 = pl.multiple_of(c * tk, tk)
        s = _dot(k_ref[pl.ds(k0, tk), :], qs_ref[...])
        if masked:
            kpos = k0 + lax.broadcasted_iota(jnp.int32, s.shape, 0)
            qpos = q_start + (lax.broadcasted_iota(jnp.int32, s.shape, 1) & (tq - 1))
            s = jnp.where(kpos <= qpos, s, NEG_INF)
        alpha, pb = _softmax_step(s, m_ref, l_ref)
        for h in range(DA_HEADS):
            cols = slice(2 * h * tq, (2 * h + 2) * tq)
            pv = _dot(vt_ref[c, (h // 2) * LANES:(h // 2 + 1) * LANES, :], pb[:, cols])
            acc_ref[:, cols] = alpha[:, cols] * acc_ref[:, cols] + pv

    n_full = q_start // tk
    lax.fori_loop(0, n_full, lambda c, carry: (step(c, False), carry)[1], 0)
    step(n_full, True)

    lam = lam_ref[0]
    inv_l = 1.0 / l_ref[...]
    g = g_ref[...]
    slabs = []
    for pair in range(DA_HEADS // 2):
        halves = []
        for h in (2 * pair, 2 * pair + 1):
            c0, c1 = slice(2 * h * tq, (2 * h + 1) * tq), slice((2 * h + 1) * tq, (2 * h + 2) * tq)
            o = acc_ref[:, c0] * inv_l[:, c0] - lam * (acc_ref[:, c1] * inv_l[:, c1])
            halves.append(_rows_subln(o[(h % 2) * MB_DH:(h % 2 + 1) * MB_DH], g, lam_init))
        slabs.append(jnp.concatenate(halves, axis=0).T)
    o_ref[...] = jnp.concatenate(slabs, axis=1).astype(BF16)


def _diff_attn_prompt(lam, q, k, vt, g_rows, n_seq, seq, lam_init):
    tq, tk = DIFF_TQ, DIFF_TK
    nq = seq // tq
    cols = 2 * DA_HEADS * tq
    return pl.pallas_call(
        functools.partial(_diff_attn_kernel, tq=tq, tk=tk, lam_init=lam_init),
        grid=(n_seq, nq),
        in_specs=[pl.BlockSpec(memory_space=pltpu.SMEM),
                  pl.BlockSpec((tq, BRANCH_W), lambda b, i: (b * nq + i, 0)),
                  pl.BlockSpec((seq, BRANCH_W), lambda b, i: (b, 0)),
                  pl.BlockSpec((seq // tk, BRANCH_W, tk), lambda b, i: (b, 0, 0)),
                  pl.BlockSpec((MB_DH, tq), lambda b, i: (0, 0))],
        out_specs=pl.BlockSpec((tq, BRANCH_W), lambda b, i: (b * nq + i, 0)),
        out_shape=jax.ShapeDtypeStruct((n_seq * seq, BRANCH_W), BF16),
        scratch_shapes=[pltpu.VMEM((BRANCH_W, cols), BF16),
                        pltpu.VMEM((1, cols), F32), pltpu.VMEM((1, cols), F32),
                        pltpu.VMEM((LANES, cols), F32)],
        compiler_params=_params(("parallel", "arbitrary")), name="diff_attn_prompt",
    )(lam, q, k, vt, g_rows)


def _top3_bits(gs, n_valid):
    blk = lax.broadcasted_iota(jnp.int32, gs.shape, 0)
    blk_f = blk.astype(F32)
    gs = jnp.where(blk < n_valid, gs, NEG_INF)
    bits = jnp.zeros((1, gs.shape[1]), jnp.int32)
    for _ in range(MOBA_TOPK):
        mx = jnp.max(gs, axis=0, keepdims=True)
        idx = jnp.min(jnp.where(gs == mx, blk_f, 1e9), axis=0, keepdims=True).astype(jnp.int32)
        bits = bits | jnp.where(idx < n_valid, jnp.left_shift(1, idx), 0)
        gs = jnp.where(blk == idx, -3e38, gs)
    return bits


def _moba_attn_kernel(q_ref, k_ref, vt_ref, o_ref, qs_ref, km_ref, m_ref, l_ref, acc_ref, *, tq, n_blocks):
    i = pl.program_id(1)
    q_start = i * tq
    own = q_start // MOBA_BLOCK
    tk = MOBA_BLOCK

    @pl.when(i == 0)
    def _():
        for j in range(n_blocks):
            blk = k_ref[j * MOBA_BLOCK:(j + 1) * MOBA_BLOCK, :].astype(F32)
            km_ref[j:j + 1, :] = jnp.mean(blk, axis=0, keepdims=True)

    qt = _stack_heads_t(q_ref[...], MB_HEADS, MB_DH)
    qs_ref[...] = qt.astype(BF16)
    bits = _top3_bits(_dot_f32(km_ref[...], qt), own)
    m_ref[...] = jnp.full(m_ref.shape, NEG_INF, F32)
    l_ref[...] = jnp.zeros(l_ref.shape, F32)
    acc_ref[...] = jnp.zeros(acc_ref.shape, F32)

    def step(c, diagonal):
        k0 = pl.multiple_of(c * tk, tk)
        s = _dot(k_ref[pl.ds(k0, tk), :], qs_ref[...])
        if diagonal:
            kpos = k0 + lax.broadcasted_iota(jnp.int32, s.shape, 0)
            qpos = q_start + (lax.broadcasted_iota(jnp.int32, s.shape, 1) & (tq - 1))
            keep = kpos <= qpos
        else:
            keep = (jnp.right_shift(bits, c) & 1) == 1
        alpha, pb = _softmax_step(jnp.where(keep, s, NEG_INF), m_ref, l_ref)
        for h in range(MB_HEADS):
            cols = slice(h * tq, (h + 1) * tq)
            pv = _dot(vt_ref[c, (h // 2) * LANES:(h // 2 + 1) * LANES, :], pb[:, cols])
            acc_ref[:, cols] = alpha[:, cols] * acc_ref[:, cols] + pv

    step(own, True)
    lax.fori_loop(0, own, lambda c, carry: (step(c, False), carry)[1], 0)

    inv_l = 1.0 / l_ref[...]
    slabs = []
    for pair in range(MB_HEADS // 2):
        halves = []
        for h in (2 * pair, 2 * pair + 1):
            cols = slice(h * tq, (h + 1) * tq)
            o = acc_ref[:, cols] * inv_l[:, cols]
            halves.append(o[(h % 2) * MB_DH:(h % 2 + 1) * MB_DH])
        slabs.append(jnp.concatenate(halves, axis=0).T)
    o_ref[...] = jnp.concatenate(slabs, axis=1).astype(BF16)


def _moba_attn_prompt(q, k, vt, n_seq, seq):
    tq = MOBA_TQ
    nq = seq // tq
    n_blocks = seq // MOBA_BLOCK
    assert n_blocks <= 31 and tq <= MOBA_BLOCK
    cols = MB_HEADS * tq
    return pl.pallas_call(
        functools.partial(_moba_attn_kernel, tq=tq, n_blocks=n_blocks),
        grid=(n_seq, nq),
        in_specs=[pl.BlockSpec((tq, BRANCH_W), lambda b, i: (b * nq + i, 0)),
                  pl.BlockSpec((seq, BRANCH_W), lambda b, i: (b, 0)),
                  pl.BlockSpec((n_blocks, BRANCH_W, MOBA_BLOCK), lambda b, i: (b, 0, 0))],
        out_specs=pl.BlockSpec((tq, BRANCH_W), lambda b, i: (b * nq + i, 0)),
        out_shape=jax.ShapeDtypeStruct((n_seq * seq, BRANCH_W), BF16),
        scratch_shapes=[pltpu.VMEM((BRANCH_W, cols), BF16),
                        pltpu.VMEM((n_blocks, BRANCH_W), F32),
                        pltpu.VMEM((1, cols), F32), pltpu.VMEM((1, cols), F32),
                        pltpu.VMEM((LANES, cols), F32)],
        compiler_params=_params(("arbitrary", "arbitrary")), name="moba_attn_prompt",
    )(q, k, vt)


PAGES_PER_STEP = 16


def _cache_pages(cache):
    depth, pool, page, n_heads, dh = cache.shape
    return jnp.transpose(cache, (0, 1, 3, 4, 2)).reshape(depth, pool, n_heads * dh, page)


def _page_specs(n_pages_step, layer):
    def spec(j):
        return pl.BlockSpec((1, 1, BRANCH_W, PAGE_SIZE),
                            lambda b, p, pt, *_: (layer, pt[b, p * n_pages_step + j], 0, 0))
    return [spec(j) for j in range(n_pages_step)]


def _hi_lo_rows(x):
    hi = x.astype(BF16).astype(F32)
    return jnp.concatenate([hi, x - hi], axis=0).astype(BF16)


def _fold_rows(y):
    return y[0:SUBLANES] + y[SUBLANES:2 * SUBLANES]


def _scores_split(q16, page):
    k_hi, k_lo = _split_bf16(page)
    return _fold_rows(_dot(q16, k_hi) + _dot(q16, k_lo))


def _values_split(p16, page):
    v_hi, v_lo = _split_bf16(page)
    return _fold_rows(_dot_nt(p16, v_hi) + _dot_nt(p16, v_lo))


def _diff_decode_kernel(pt_ref, lam_ref, q_ref, kn_ref, vn_ref, g_ref, *refs, pps, lam_init):
    k_refs, v_refs = refs[:pps], refs[pps:2 * pps]
    o_ref, qs_ref, m_ref, l_ref, acc_ref = refs[2 * pps:]
    p_idx = pl.program_id(1)
    n_stack = 2 * DA_HEADS

    @pl.when(p_idx == 0)
    def _():
        grp = lax.broadcasted_iota(jnp.int32, (n_stack, BRANCH_W), 1) // DA_DQK
        row = lax.broadcasted_iota(jnp.int32, (n_stack, BRANCH_W), 0)
        qs = jnp.where(grp == row, jnp.broadcast_to(q_ref[0], (n_stack, BRANCH_W)), 0.0)
        qs_ref[...] = _hi_lo_rows(qs)
        m_ref[...] = jnp.sum(qs * kn_ref[0], axis=1, keepdims=True)
        l_ref[...] = jnp.ones(l_ref.shape, F32)
        acc_ref[...] = jnp.broadcast_to(vn_ref[0], acc_ref.shape)

    q16 = qs_ref[...]
    s = jnp.concatenate([_scores_split(q16, k_refs[j][0, 0]) for j in range(pps)], axis=1)
    m_old = m_ref[...]
    m_new = jnp.maximum(m_old, jnp.max(s, axis=1, keepdims=True))
    alpha = jnp.exp(m_old - m_new)
    p = jnp.exp(s - m_new)
    l_ref[...] = alpha * l_ref[...] + jnp.sum(p, axis=1, keepdims=True)
    m_ref[...] = m_new
    p16 = _hi_lo_rows(p)
    pv = _values_split(p16[:, 0:PAGE_SIZE], v_refs[0][0, 0])
    for j in range(1, pps):
        pv = pv + _values_split(p16[:, j * PAGE_SIZE:(j + 1) * PAGE_SIZE], v_refs[j][0, 0])
    acc_ref[...] = alpha * acc_ref[...] + pv

    @pl.when(p_idx == pl.num_programs(1) - 1)
    def _():
        lam = lam_ref[0]
        o = acc_ref[...] / l_ref[...]
        grp = lax.broadcasted_iota(jnp.int32, (1, BRANCH_W), 1) // MB_DH
        out = jnp.zeros((1, BRANCH_W), F32)
        for h in range(DA_HEADS):
            out = out + jnp.where(grp == h, o[2 * h:2 * h + 1] - lam * o[2 * h + 1:2 * h + 2], 0.0)
        lo, hi = out[:, 0:LANES], out[:, LANES:2 * LANES]
        g_pair = g_ref[...]
        o_ref[0] = jnp.concatenate([_subln(lo, g_pair, lam_init), _subln(hi, g_pair, lam_init)], axis=1)


def _diff_attn_decode(page_table, lam, q, k_new, v_new, g_pair, pages_k, pages_v, layer, lam_init):
    n_seq, n_pages = page_table.shape
    pps = min(PAGES_PER_STEP, n_pages)
    row3 = pl.BlockSpec((1, 1, BRANCH_W), lambda b, p, pt: (b, 0, 0))
    n_stack = 2 * DA_HEADS
    out = pl.pallas_call(
        functools.partial(_diff_decode_kernel, pps=pps, lam_init=lam_init),
        grid_spec=pltpu.PrefetchScalarGridSpec(
            num_scalar_prefetch=1, grid=(n_seq, n_pages // pps),
            in_specs=[pl.BlockSpec(memory_space=pltpu.SMEM), row3, row3, row3,
                      pl.BlockSpec((1, LANES), lambda b, p, pt: (0, 0))] +
                     _page_specs(pps, layer) + _page_specs(pps, layer),
            out_specs=row3,
            scratch_shapes=[pltpu.VMEM((2 * n_stack, BRANCH_W), BF16), pltpu.VMEM((n_stack, 1), F32),
                            pltpu.VMEM((n_stack, 1), F32), pltpu.VMEM((n_stack, BRANCH_W), F32)]),
        out_shape=jax.ShapeDtypeStruct((n_seq, 1, BRANCH_W), F32),
        compiler_params=_params(("arbitrary", "arbitrary")), name="diff_attn_decode",
    )(page_table, lam, q.reshape(n_seq, 1, BRANCH_W), k_new.reshape(n_seq, 1, BRANCH_W),
      v_new.reshape(n_seq, 1, BRANCH_W), g_pair, *([pages_k] * pps), *([pages_v] * pps))
    return out.reshape(n_seq, BRANCH_W)


def _moba_gate_kernel(pt_ref, q_ref, *refs, pps):
    k_refs, o_ref = refs[:pps], refs[pps]
    ppb = MOBA_BLOCK // PAGE_SIZE
    q_col = q_ref[0]
    for j in range(pps // ppb):
        tot = k_refs[ppb * j][0, 0]
        for t in range(1, ppb):
            tot = tot + k_refs[ppb * j + t][0, 0]
        prod = tot * q_col
        rows = [jnp.sum(prod[h * MB_DH:(h + 1) * MB_DH], axis=0, keepdims=True) for h in range(MB_HEADS)]
        rows.append(jnp.zeros((SUBLANES - MB_HEADS, PAGE_SIZE), F32))
        o_ref[0, j] = jnp.concatenate(rows, axis=0) * (1.0 / MOBA_BLOCK)


def _moba_select_kernel(g_ref, idx_ref, *, n_blocks):
    g = jnp.sum(g_ref[0], axis=1, keepdims=True)
    r = lax.broadcasted_iota(jnp.int32, g.shape, 0)
    r_f = r.astype(F32)
    out_row = lax.broadcasted_iota(jnp.int32, (SUBLANES, LANES), 0)
    out_lane = lax.broadcasted_iota(jnp.int32, (SUBLANES, LANES), 1)
    out = jnp.zeros((SUBLANES, LANES), jnp.int32)
    for h in range(MB_HEADS):
        gh = jnp.where((r & (SUBLANES - 1)) == h, g, -3e38)
        for t in range(MOBA_TOPK):
            mx = jnp.max(gh, axis=0, keepdims=True)
            idx = jnp.min(jnp.where(gh == mx, r_f, 1e9), axis=0, keepdims=True).astype(jnp.int32)
            gh = jnp.where(r == idx, -3e38, gh)
            out = jnp.where((out_row == t) & (out_lane == h), jnp.right_shift(idx, 3), out)
    idx_ref[0] = out


def _moba_decode_kernel(pt_ref, top_ref, q_ref, kn_ref, vn_ref, *refs, n_sel):
    k_refs, v_refs, o_ref = refs[:n_sel], refs[n_sel:2 * n_sel], refs[-1]
    qh = q_ref[0, 0]
    s_new = jnp.sum(qh * kn_ref[0, 0], axis=1, keepdims=True)
    q16 = _hi_lo_rows(jnp.broadcast_to(qh, (SUBLANES, MB_DH)))
    s = jnp.concatenate([_scores_split(q16, k_refs[j][0, 0]) for j in range(n_sel)], axis=1)[0:1]
    m = jnp.maximum(s_new, jnp.max(s, axis=1, keepdims=True))
    p_new = jnp.exp(s_new - m)
    p = jnp.exp(s - m)
    p16 = _hi_lo_rows(jnp.broadcast_to(p, (SUBLANES, n_sel * PAGE_SIZE)))
    pv = p_new * vn_ref[0, 0]
    for j in range(n_sel):
        pv = pv + _values_split(p16[:, j * PAGE_SIZE:(j + 1) * PAGE_SIZE], v_refs[j][0, 0])[0:1]
    o_ref[0, 0] = pv / (p_new + jnp.sum(p, axis=1, keepdims=True))


def _moba_attn_decode(page_table, q, k_new, v_new, pages_k, pages_v, layer):
    n_seq, n_pages = page_table.shape
    pps = min(PAGES_PER_STEP, n_pages)
    ppb = MOBA_BLOCK // PAGE_SIZE
    n_blocks = n_pages // ppb
    assert n_blocks >= MOBA_TOPK and pps % ppb == 0
    gate = pl.pallas_call(
        functools.partial(_moba_gate_kernel, pps=pps),
        grid_spec=pltpu.PrefetchScalarGridSpec(
            num_scalar_prefetch=1, grid=(n_seq, n_pages // pps),
            in_specs=[pl.BlockSpec((1, BRANCH_W, 1), lambda b, p, pt: (b, 0, 0))] + _page_specs(pps, layer),
            out_specs=pl.BlockSpec((1, pps // ppb, SUBLANES, PAGE_SIZE), lambda b, p, pt: (b, p, 0, 0))),
        out_shape=jax.ShapeDtypeStruct((n_seq, n_blocks, SUBLANES, PAGE_SIZE), F32),
        compiler_params=_params(("arbitrary", "arbitrary")), name="moba_gate_decode",
    )(page_table, q.reshape(n_seq, BRANCH_W, 1), *([pages_k] * pps))
    top = pl.pallas_call(
        functools.partial(_moba_select_kernel, n_blocks=n_blocks),
        grid=(n_seq,),
        in_specs=[pl.BlockSpec((1, n_blocks * SUBLANES, PAGE_SIZE), lambda b: (b, 0, 0))],
        out_specs=pl.BlockSpec((1, SUBLANES, LANES), lambda b: (b, 0, 0)),
        out_shape=jax.ShapeDtypeStruct((n_seq, SUBLANES, LANES), jnp.int32),
        compiler_params=_params(("parallel",)), name="moba_select_decode",
    )(gate.reshape(n_seq, n_blocks * SUBLANES, PAGE_SIZE))
    top = top[:, :MOBA_TOPK, :MB_HEADS].transpose(0, 2, 1).reshape(-1)
    head = pl.BlockSpec((1, 1, 1, MB_DH), lambda b, h, pt, tp: (b, h, 0, 0))
    n_sel = MOBA_TOPK * ppb

    def page(j):
        return pl.BlockSpec(
            (1, 1, MB_DH, PAGE_SIZE),
            lambda b, h, pt, tp: (layer, pt[b, tp[(b * MB_HEADS + h) * MOBA_TOPK + j // ppb] * ppb + j % ppb], h, 0))

    pages = [page(j) for j in range(n_sel)]
    per_head = lambda a: a.reshape(n_seq, MB_HEADS, 1, MB_DH)
    o = pl.pallas_call(
        functools.partial(_moba_decode_kernel, n_sel=n_sel),
        grid_spec=pltpu.PrefetchScalarGridSpec(
            num_scalar_prefetch=2, grid=(n_seq, MB_HEADS),
            in_specs=[head, head, head] + pages + pages,
            out_specs=head),
        out_shape=jax.ShapeDtypeStruct((n_seq, MB_HEADS, 1, MB_DH), F32),
        compiler_params=_params(("arbitrary", "arbitrary")), name="moba_attn_decode",
    )(page_table, top, per_head(q), per_head(k_new), per_head(v_new),
      *([pages_k] * len(pages)), *([pages_v] * len(pages)))
    return o.reshape(n_seq, BRANCH_W)


def _merge_kernel(x_ref, g_ref, oa_ref, ob_ref, oc_ref, od_ref, wgl_ref, wbr_ref, wout_ref, xo_ref):
    x = x_ref[...]
    hb = _rms(x, g_ref[...]).astype(BF16)
    merged = None
    for k, br in enumerate((oa_ref, ob_ref, oc_ref, od_ref)):
        gate = jax.nn.sigmoid(_dot(hb, wgl_ref[:, k * D_MODEL:(k + 1) * D_MODEL]))
        term = gate * _dot(br[...], wbr_ref[k])
        merged = term if merged is None else merged + term
    xo_ref[...] = x + _dot(merged.astype(BF16), wout_ref[...])


def _merge(x, g, oa, ob, oc, od, w_gl, w_br, w_out):
    n = x.shape[0]
    tm = min(512, n)
    row = lambda i: (i, 0)
    const2 = lambda i: (0, 0)
    br_spec = pl.BlockSpec((tm, BRANCH_W), row)
    return pl.pallas_call(
        _merge_kernel, grid=(n // tm,),
        in_specs=[pl.BlockSpec((tm, D_MODEL), row), pl.BlockSpec((1, D_MODEL), const2)] + [br_spec] * 4 +
                 [pl.BlockSpec((D_MODEL, N_BRANCH * D_MODEL), const2),
                  pl.BlockSpec((N_BRANCH, BRANCH_W, D_MODEL), lambda i: (0, 0, 0)),
                  pl.BlockSpec((D_MODEL, D_MODEL), const2)],
        out_specs=pl.BlockSpec((tm, D_MODEL), row),
        out_shape=jax.ShapeDtypeStruct((n, D_MODEL), F32),
        compiler_params=_params(("parallel",)), name="merge",
    )(x, g, oa, ob, oc, od, w_gl, w_br, w_out)


def _merge_decode_kernel(x_ref, g_ref, br_ref, wgl_ref, wbr_ref, wout_ref, xo_ref, acc_ref):
    k = pl.program_id(0)
    x = x_ref[...]
    h = _rms(x, g_ref[...])
    term = jax.nn.sigmoid(_dot_f32(h, wgl_ref[...])) * _dot_f32(br_ref[0], wbr_ref[0])

    @pl.when(k == 0)
    def _():
        acc_ref[...] = term

    @pl.when(k > 0)
    def _():
        acc_ref[...] += term

    @pl.when(k == N_BRANCH - 1)
    def _():
        xo_ref[...] = x + _dot_f32(acc_ref[...], wout_ref[...])


def _merge_decode(x, g, branches, w_gl, w_br, w_out):
    n = x.shape[0]
    const2 = lambda k: (0, 0)
    return pl.pallas_call(
        _merge_decode_kernel, grid=(N_BRANCH,),
        in_specs=[pl.BlockSpec((n, D_MODEL), const2), pl.BlockSpec((1, D_MODEL), const2),
                  pl.BlockSpec((1, n, BRANCH_W), lambda k: (k, 0, 0)),
                  pl.BlockSpec((D_MODEL, D_MODEL), lambda k: (0, k)),
                  pl.BlockSpec((1, BRANCH_W, D_MODEL), lambda k: (k, 0, 0)),
                  pl.BlockSpec((D_MODEL, D_MODEL), const2)],
        out_specs=pl.BlockSpec((n, D_MODEL), const2),
        out_shape=jax.ShapeDtypeStruct((n, D_MODEL), F32),
        scratch_shapes=[pltpu.VMEM((n, D_MODEL), F32)],
        compiler_params=_params(("arbitrary",)), name="merge_decode",
    )(x, g, branches, w_gl, w_br, w_out)


def _router_kernel(x_ref, g_ref, w_ref, b_ref, h_ref, comb_ref):
    h = _rms(x_ref[...], g_ref[...])
    h_ref[...] = h.astype(h_ref.dtype)
    lg = _dot_f32(h, w_ref[...]) + b_ref[...]
    lane = lax.broadcasted_iota(jnp.int32, lg.shape, 1)
    lane_f = lane.astype(F32)
    is_g = (lane >= N_EXPERTS) & (lane < N_EXPERTS + N_EGROUPS)
    mxg = jnp.max(jnp.where(is_g, lg, NEG_INF), axis=1, keepdims=True)
    gsel = jnp.min(jnp.where(is_g & (lg == mxg), lane_f, 1e9), axis=1, keepdims=True).astype(jnp.int32)
    gp = 1.0 / jnp.sum(jnp.where(is_g, jnp.exp(lg - mxg), 0.0), axis=1, keepdims=True)
    e_lo = (gsel - N_EXPERTS) * E_PER_GROUP
    in_g = (lane >= e_lo) & (lane < e_lo + E_PER_GROUP)
    v1 = jnp.max(jnp.where(in_g, lg, NEG_INF), axis=1, keepdims=True)
    i1 = jnp.min(jnp.where(in_g & (lg == v1), lane_f, 1e9), axis=1, keepdims=True).astype(jnp.int32)
    rest = in_g & (lane != i1)
    v2 = jnp.max(jnp.where(rest, lg, NEG_INF), axis=1, keepdims=True)
    i2 = jnp.min(jnp.where(rest & (lg == v2), lane_f, 1e9), axis=1, keepdims=True).astype(jnp.int32)
    e2 = jnp.exp(v2 - v1)
    w1 = gp / (1.0 + e2)
    w2 = gp * e2 / (1.0 + e2)
    comb_ref[...] = jnp.where(lane == i1, w1, 0.0) + jnp.where(lane == i2, w2, 0.0)


def _router(x, g, w_router, b_router, h_dtype):
    n = x.shape[0]
    tm = min(512, n)
    row = lambda i: (i, 0)
    const2 = lambda i: (0, 0)
    return pl.pallas_call(
        _router_kernel, grid=(n // tm,),
        in_specs=[pl.BlockSpec((tm, D_MODEL), row), pl.BlockSpec((1, D_MODEL), const2),
                  pl.BlockSpec((D_MODEL, LANES), const2), pl.BlockSpec((1, LANES), const2)],
        out_specs=[pl.BlockSpec((tm, D_MODEL), row), pl.BlockSpec((tm, LANES), row)],
        out_shape=[jax.ShapeDtypeStruct((n, D_MODEL), h_dtype), jax.ShapeDtypeStruct((n, LANES), F32)],
        compiler_params=_params(("parallel",)), name="router",
    )(x, g, w_router, b_router)


def _experts_kernel(x_ref, h_ref, comb_ref, wg_ref, wu_ref, wd_ref, gf_ref, xo_ref, acc_ref,
                    *, final_norm, precise):
    e = pl.program_id(1)
    dot = _dot_f32 if precise else _dot

    @pl.when(e == 0)
    def _():
        acc_ref[...] = x_ref[...]

    comb = comb_ref[...]
    lane = lax.broadcasted_iota(jnp.int32, comb.shape, 1)
    cw = jnp.sum(jnp.where(lane == e, comb, 0.0), axis=1, keepdims=True)
    hb = h_ref[...]
    a = jax.nn.silu(dot(hb, wg_ref[0])) * dot(hb, wu_ref[0])
    acc_ref[...] += dot((a * cw).astype(hb.dtype), wd_ref[0])

    @pl.when(e == pl.num_programs(1) - 1)
    def _():
        xo = acc_ref[...]
        xo_ref[...] = _rms(xo, gf_ref[...]) if final_norm else xo


def _experts(x, h, comb, w_gate, w_up, w_down, g_final, final_norm, precise):
    n = x.shape[0]
    tm = min(1024, n)
    row = lambda i, e: (i, 0)
    return pl.pallas_call(
        functools.partial(_experts_kernel, final_norm=final_norm, precise=precise),
        grid=(n // tm, N_EXPERTS),
        in_specs=[pl.BlockSpec((tm, D_MODEL), row), pl.BlockSpec((tm, D_MODEL), row),
                  pl.BlockSpec((tm, LANES), row),
                  pl.BlockSpec((1, D_MODEL, D_EXPERT), lambda i, e: (e, 0, 0)),
                  pl.BlockSpec((1, D_MODEL, D_EXPERT), lambda i, e: (e, 0, 0)),
                  pl.BlockSpec((1, D_EXPERT, D_MODEL), lambda i, e: (e, 0, 0)),
                  pl.BlockSpec((1, D_MODEL), lambda i, e: (0, 0))],
        out_specs=pl.BlockSpec((tm, D_MODEL), row),
        out_shape=jax.ShapeDtypeStruct((n, D_MODEL), F32),
        scratch_shapes=[pltpu.VMEM((tm, D_MODEL), F32)],
        compiler_params=_params(("parallel", "arbitrary")), name="experts",
    )(x, h, comb, w_gate, w_up, w_down, g_final)


def _layer_weights(l, W):
    w_in = W["w_in"][l]
    lam_init = 0.8 - 0.6 * math.exp(-0.3 * l)
    lam = (jnp.exp(jnp.sum(W["lambda_q1"][l] * W["lambda_k1"][l]))
           - jnp.exp(jnp.sum(W["lambda_q2"][l] * W["lambda_k2"][l])) + lam_init).reshape(1).astype(F32)
    tril = jnp.tril(jnp.ones((CHUNK, CHUNK), F32))
    ws, bs = W["chunk_ws"][l], W["chunk_b"][l]
    rep = BRANCH_W // CH_GROUPS
    router_w = jnp.concatenate(
        [W["router_e_w"][l].transpose(1, 0, 2).reshape(D_MODEL, N_EXPERTS), W["router_g_w"][l],
         jnp.zeros((D_MODEL, LANES - N_EXPERTS - N_EGROUPS), F32)], axis=1)
    router_b = jnp.concatenate(
        [W["router_e_b"][l].reshape(N_EXPERTS), W["router_g_b"][l],
         jnp.zeros((LANES - N_EXPERTS - N_EGROUPS,), F32)]).reshape(1, LANES)
    f32w = dict(
        w_qkv=w_in[:, :N_QKV_COLS], w_gl=w_in[:, N_QKV_COLS:], w_br=W["w_branch"][l], w_out=W["w_out"][l],
        w_gate=W["w_gate_e"][l].reshape(N_EXPERTS, D_MODEL, D_EXPERT),
        w_up=W["w_up_e"][l].reshape(N_EXPERTS, D_MODEL, D_EXPERT),
        w_down=W["w_down_e"][l].reshape(N_EXPERTS, D_EXPERT, D_MODEL))
    return dict(
        lam_init=lam_init, lam=lam,
        g_mix=W["norm_mix_g"][l].reshape(1, D_MODEL), g_ffn=W["norm_ffn_g"][l].reshape(1, D_MODEL),
        conv_w=W["conv_w"][l],
        wsc=(ws * tril).astype(BF16), bsf=jnp.repeat(bs.T, rep, axis=1),
        wdiag=jnp.repeat(ws[:, 0, 0], rep).reshape(1, BRANCH_W), b0=jnp.repeat(bs[:, 0], rep).reshape(1, BRANCH_W),
        g_sub=jnp.tile(W["subln_g"][l], LANES // MB_DH).reshape(1, LANES),
        g_rows=jnp.broadcast_to(W["subln_g"][l][:, None], (MB_DH, DIFF_TQ)),
        router_w=router_w, router_b=router_b,
        f32=f32w, bf16={k: v.astype(BF16) for k, v in f32w.items()},
    )


def _ffn(x, lw, g_final, final_norm, precise):
    w = lw["f32"] if precise else lw["bf16"]
    h, comb = _router(x, lw["g_ffn"], lw["router_w"], lw["router_b"], F32 if precise else BF16)
    return _experts(x, h, comb, w["w_gate"], w["w_up"], w["w_down"], g_final, final_norm, precise)


def _run_prompt(x_prompt, LW, g_final):
    n_seq, seq, _ = x_prompt.shape
    x = x_prompt.reshape(n_seq * seq, D_MODEL)
    tabs = _rope_tables(seq, 0, 1)
    states = []
    for l, lw in enumerate(LW):
        w = lw["bf16"]
        (ak, av, mk, mv, conv, qa, ka, va, qm, km, vm, ob, od) = _inproj_prompt(
            x, lw["g_mix"], w["w_qkv"], tabs, lw["conv_w"], lw["wsc"], lw["bsf"], n_seq, seq)
        oa = _diff_attn_prompt(lw["lam"], qa, ka, va, lw["g_rows"], n_seq, seq, lw["lam_init"])
        oc = _moba_attn_prompt(qm, km, vm, n_seq, seq)
        x = _merge(x, lw["g_mix"], oa, ob, oc, od, w["w_gl"], w["w_br"], w["w_out"])
        x = _ffn(x, lw, g_final, l == len(LW) - 1, False)
        shp = (n_seq, seq, DA_HEADS, 2 * DA_DQK)
        states.append((ak.reshape(shp), av.reshape(shp), mk.reshape(shp), mv.reshape(shp), conv[:, 6:8]))
    return x.reshape(n_seq, seq, D_MODEL), states


def _run_sample(x_sample, caches, state_conv, page_table, LW, g_final):
    n_seq = x_sample.shape[0]
    past_len = page_table.shape[1] * PAGE_SIZE
    x = x_sample.reshape(n_seq, D_MODEL)
    tabs = _rope_tables(n_seq, past_len, 0)
    pk_d, pv_d, pk_m, pv_m = [_cache_pages(c) for c in caches]
    states = []
    for l, lw in enumerate(LW):
        w = lw["f32"]
        s0, s1 = state_conv[l][:, 0], state_conv[l][:, 1]
        (ak, av, mk, mv, dv, z, qa, qm, ob, od) = _inproj_decode(
            x, lw["g_mix"], w["w_qkv"], tabs, lw["conv_w"], lw["wdiag"], lw["b0"], s0, s1)
        oa = _diff_attn_decode(page_table, lw["lam"], qa, ak, av, lw["g_sub"], pk_d, pv_d, l, lw["lam_init"])
        oc = _moba_attn_decode(page_table, qm, mk, mv, pk_m, pv_m, l)
        x = _merge_decode(x, lw["g_mix"], jnp.stack([oa, ob, oc, od]), w["w_gl"], w["w_br"], w["w_out"])
        x = _ffn(x, lw, g_final, l == len(LW) - 1, True)
        shp = (n_seq, 1, DA_HEADS, 2 * DA_DQK)
        states.append((ak.reshape(shp), av.reshape(shp), mk.reshape(shp), mv.reshape(shp),
                       jnp.stack([s1, z], axis=1), dv.reshape(n_seq, 1, BRANCH_W)))
    return x.reshape(n_seq, 1, D_MODEL), states


def kernel(x_prompt, x_sample, cache_k_diff, cache_v_diff, cache_k_moba, cache_v_moba, state_conv, page_table, norm_mix_g, w_in, lambda_q1, lambda_k1, lambda_q2, lambda_k2, subln_g, conv_w, chunk_ws, chunk_b, w_branch, w_out, norm_ffn_g, router_g_w, router_g_b, router_e_w, router_e_b, w_gate_e, w_up_e, w_down_e, norm_final_g):
    W = dict(norm_mix_g=norm_mix_g, w_in=w_in, lambda_q1=lambda_q1, lambda_k1=lambda_k1,
             lambda_q2=lambda_q2, lambda_k2=lambda_k2, subln_g=subln_g, conv_w=conv_w,
             chunk_ws=chunk_ws, chunk_b=chunk_b, w_branch=w_branch, w_out=w_out, norm_ffn_g=norm_ffn_g,
             router_g_w=router_g_w, router_g_b=router_g_b, router_e_w=router_e_w, router_e_b=router_e_b,
             w_gate_e=w_gate_e, w_up_e=w_up_e, w_down_e=w_down_e)
    depth = w_in.shape[0]
    LW = [_layer_weights(l, W) for l in range(depth)]
    g_final = norm_final_g.reshape(1, D_MODEL)
    y_prompt, sp = _run_prompt(x_prompt, LW, g_final)
    y_sample, ss = _run_sample(x_sample, (cache_k_diff, cache_v_diff, cache_k_moba, cache_v_moba),
                               state_conv, page_table, LW, g_final)
    stack = lambda st, k: jnp.stack([s[k] for s in st])
    return (y_prompt, y_sample,
            stack(sp, 0), stack(sp, 1), stack(sp, 2), stack(sp, 3), stack(sp, 4),
            stack(ss, 0), stack(ss, 1), stack(ss, 2), stack(ss, 3), stack(ss, 4), stack(ss, 5))
```

```python
import functools
import math

import jax
import jax.numpy as jnp
from jax import lax
from jax.experimental import pallas as pl
from jax.experimental.pallas import tpu as pltpu

F32 = jnp.float32
BF16 = jnp.bfloat16

D_MODEL = 1024
BRANCH_W = 256
N_BRANCH = 4
DA_HEADS, DA_DQK = 4, 32
MB_HEADS, MB_DH = 4, 64
MOBA_BLOCK = 256
MOBA_TOPK = 3
CHUNK = 128
CH_GROUPS = 4
N_EGROUPS, E_PER_GROUP = 4, 8
N_EXPERTS = N_EGROUPS * E_PER_GROUP
D_EXPERT = 256
PAGE_SIZE = 128
ROPE_THETA = 10000.0
NORM_EPS = 1e-6
NEG_INF = -1e30
N_QKV_COLS = 11 * BRANCH_W
LOG2_E = math.log2(math.e)
DIFF_TQ, DIFF_TK = 128, 512
MOBA_TQ = 256
MOE_TM = 256
ROUTE_ID_LANE, ROUTE_W_LANE = 64, 66

LANES = 128
SUBLANES = 8
VMEM_LIMIT = 48 * 1024 * 1024


def _params(sem):
    return pltpu.CompilerParams(dimension_semantics=sem, vmem_limit_bytes=VMEM_LIMIT)


def _rms(x, g):
    return x * lax.rsqrt(jnp.mean(x * x, axis=-1, keepdims=True) + NORM_EPS) * g


def _dot(a, b):
    return jnp.dot(a, b, preferred_element_type=F32)


def _dot_nt(a, b):
    return lax.dot_general(a, b, (((1,), (1,)), ((), ())), preferred_element_type=F32)


def _dot_f32(a, b):
    return jnp.dot(a, b, precision=lax.Precision.HIGHEST, preferred_element_type=F32)


def _split_bf16(x):
    hi = x.astype(BF16)
    return hi, (x - hi.astype(F32)).astype(BF16)


def _rope_table_kernel(ca_ref, sa_ref, cm_ref, sm_ref, *, pos0, step, rows):
    i = pl.program_id(0)
    r = lax.broadcasted_iota(jnp.int32, (rows, BRANCH_W), 0)
    c = lax.broadcasted_iota(jnp.int32, (rows, BRANCH_W), 1)
    pos = (pos0 + (i * rows + r) * step).astype(F32)

    def tables(d):
        half = d // 2
        ci = c & (d - 1)
        fi = (ci & (half - 1)).astype(F32)
        inv = jnp.exp((-math.log(ROPE_THETA) * fi) * (2.0 / d))
        ang = pos * inv
        sign = jnp.where(ci < half, -1.0, 1.0).astype(F32)
        return jnp.cos(ang), jnp.sin(ang) * sign

    ca_ref[...], sa_ref[...] = tables(DA_DQK)
    cm_ref[...], sm_ref[...] = tables(MB_DH)


def _rope_tables(n_rows, pos0, step):
    rows = min(n_rows, 512)
    spec = pl.BlockSpec((rows, BRANCH_W), lambda i: (i, 0))
    shp = jax.ShapeDtypeStruct((n_rows, BRANCH_W), F32)
    return pl.pallas_call(
        functools.partial(_rope_table_kernel, pos0=pos0, step=step, rows=rows),
        grid=(n_rows // rows,), out_specs=[spec] * 4, out_shape=[shp] * 4,
        compiler_params=_params(("parallel",)), name="rope_tables")()


def _rope(x, cos, sin_signed, d):
    half = d // 2
    c = lax.broadcasted_iota(jnp.int32, x.shape, 1)
    first = (c & (d - 1)) < half
    rot = jnp.where(first, pltpu.roll(x, BRANCH_W - half, 1), pltpu.roll(x, half, 1))
    return x * cos + rot * sin_signed


def _inproj_kernel(*refs, tm, tiles_per_seq, decode):
    if decode:
        (x_ref, g_ref, w_ref, ca_ref, sa_ref, cm_ref, sm_ref, cw_ref, wd_ref, b0_ref, s0_ref, s1_ref,
         ak_o, av_o, mk_o, mv_o, dv_o, z_o, qa_o, qm_o, ob_o, od_o) = refs
    else:
        (x_ref, g_ref, w_ref, ca_ref, sa_ref, cm_ref, sm_ref, cw_ref, wsc_ref, bsf_ref,
         ak_o, av_o, mk_o, mv_o, conv_o, qa_o, ka_o, va_o, qm_o, km_o, vm_o, ob_o, od_o, zbuf) = refs
    i = pl.program_id(0)
    act = F32 if decode else BF16
    hb = _rms(x_ref[...], g_ref[...]).astype(act)

    def proj(k):
        w = w_ref[:, k * BRANCH_W:(k + 1) * BRANCH_W]
        return _dot_f32(hb, w) if decode else _dot(hb, w)

    ca, sa, cm, sm = ca_ref[...], sa_ref[...], cm_ref[...], sm_ref[...]
    aq = _rope(proj(0), ca, sa, DA_DQK)
    ak = _rope(proj(1), ca, sa, DA_DQK)
    av = proj(2)
    exp_base = 1.0 if decode else LOG2_E
    qa_o[...] = (aq * (DA_DQK ** -0.5 * exp_base)).astype(act)
    mq = _rope(proj(6), cm, sm, MB_DH)
    mk = _rope(proj(7), cm, sm, MB_DH)
    mv = proj(8)
    qm_o[...] = (mq * (MB_DH ** -0.5 * exp_base)).astype(act)
    if decode:
        ak_o[...], av_o[...], mk_o[...], mv_o[...] = ak, av, mk, mv
    else:
        ka_o[...] = ak.astype(BF16)
        km_o[...] = mk.astype(BF16)
        av_t, mv_t = av.T, mv.T
        ak_o[0], av_o[0], mk_o[0], mv_o[0] = ak.T, av_t, mk.T, mv_t
        for c in range(tm // DIFF_TK):
            va_o[c] = av_t[:, c * DIFF_TK:(c + 1) * DIFF_TK].astype(BF16)
        for c in range(tm // MOBA_BLOCK):
            vm_o[c] = mv_t[:, c * MOBA_BLOCK:(c + 1) * MOBA_BLOCK].astype(BF16)
    sb = proj(3)
    z = proj(4) * proj(5)
    cw = cw_ref[...]
    if decode:
        y = cw[0:1] * s0_ref[...] + cw[1:2] * s1_ref[...] + cw[2:3] * z
        z_o[...] = z
    else:
        @pl.when(i % tiles_per_seq == 0)
        def _():
            zbuf[0:8, :] = jnp.zeros((8, BRANCH_W), F32)
        zbuf[8:8 + tm, :] = z
        y = cw[0:1] * zbuf[6:6 + tm, :] + cw[1:2] * zbuf[7:7 + tm, :] + cw[2:3] * z
        zbuf[0:8, :] = z[tm - 8:tm]
        conv_o[0] = z[tm - 8:tm]
    ob_o[...] = (sb * y).astype(act)
    du = proj(9)
    dv = proj(10)
    if decode:
        dv_o[...] = dv
        od_o[...] = du * (wd_ref[...] * dv + b0_ref[...])
    else:
        grp = lax.broadcasted_iota(jnp.int32, (CHUNK, BRANCH_W), 1) // (BRANCH_W // CH_GROUPS)
        for c0 in range(0, tm, CHUNK):
            dvc = dv[c0:c0 + CHUNK].astype(BF16)
            mixed = bsf_ref[...]
            for gi in range(CH_GROUPS):
                mixed = mixed + jnp.where(grp == gi, _dot(wsc_ref[gi], dvc), 0.0)
            od_o[c0:c0 + CHUNK, :] = (du[c0:c0 + CHUNK] * mixed).astype(BF16)


def _inproj_prompt(x, g, w_qkv, tabs, conv_w, wsc, bsf, n_seq, seq):
    n = x.shape[0]
    tm = min(512, seq)
    tps = seq // tm
    row = lambda i: (i, 0)
    const2 = lambda i: (0, 0)
    tab_spec = pl.BlockSpec((tm, BRANCH_W), lambda i: (i % tps, 0))
    out_spec = pl.BlockSpec((tm, BRANCH_W), row)
    state_spec = pl.BlockSpec((1, BRANCH_W, tm), lambda i: (i // tps, 0, i % tps))
    state_shape = jax.ShapeDtypeStruct((n_seq, BRANCH_W, seq), F32)
    bfo = jax.ShapeDtypeStruct((n, BRANCH_W), BF16)
    vt_spec = lambda chunk: pl.BlockSpec((tm // chunk, BRANCH_W, chunk), lambda i: (i, 0, 0))
    vt_shape = lambda chunk: jax.ShapeDtypeStruct((n // chunk, BRANCH_W, chunk), BF16)
    assert tm % DIFF_TK == 0 and tm % MOBA_BLOCK == 0
    outs = pl.pallas_call(
        functools.partial(_inproj_kernel, tm=tm, tiles_per_seq=tps, decode=False),
        grid=(n // tm,),
        in_specs=[pl.BlockSpec((tm, D_MODEL), row), pl.BlockSpec((1, D_MODEL), const2),
                  pl.BlockSpec((D_MODEL, N_QKV_COLS), const2)] + [tab_spec] * 4 +
                 [pl.BlockSpec((3, BRANCH_W), const2),
                  pl.BlockSpec((CH_GROUPS, CHUNK, CHUNK), lambda i: (0, 0, 0)),
                  pl.BlockSpec((CHUNK, BRANCH_W), const2)],
        out_specs=[state_spec] * 4 + [pl.BlockSpec((1, 8, BRANCH_W), lambda i: (i // tps, 0, 0))] +
                  [out_spec, out_spec, vt_spec(DIFF_TK), out_spec, out_spec, vt_spec(MOBA_BLOCK), out_spec, out_spec],
        out_shape=[state_shape] * 4 + [jax.ShapeDtypeStruct((n_seq, 8, BRANCH_W), F32)] +
                  [bfo, bfo, vt_shape(DIFF_TK), bfo, bfo, vt_shape(MOBA_BLOCK), bfo, bfo],
        scratch_shapes=[pltpu.VMEM((tm + 8, BRANCH_W), F32)],
        compiler_params=_params(("arbitrary",)), name="inproj_prompt",
    )(x, g, w_qkv, *tabs, conv_w, wsc, bsf)
    return outs


def _inproj_decode(x, g, w_qkv, tabs, conv_w, wdiag, b0, s0, s1):
    n = x.shape[0]
    full = lambda shape: pl.BlockSpec(shape, lambda i: (0,) * len(shape))
    nb = full((n, BRANCH_W))
    f32o = jax.ShapeDtypeStruct((n, BRANCH_W), F32)
    return pl.pallas_call(
        functools.partial(_inproj_kernel, tm=n, tiles_per_seq=1, decode=True),
        grid=(1,),
        in_specs=[full((n, D_MODEL)), full((1, D_MODEL)), full((D_MODEL, N_QKV_COLS))] + [nb] * 4 +
                 [full((3, BRANCH_W)), full((1, BRANCH_W)), full((1, BRANCH_W)), nb, nb],
        out_specs=[nb] * 10,
        out_shape=[f32o] * 10,
        compiler_params=_params(("arbitrary",)), name="inproj_decode",
    )(x, g, w_qkv, *tabs, conv_w, wdiag, b0, s0, s1)


def _stack_heads_t(q_bf16, n_stack, width):
    qt = q_bf16.astype(F32).T
    grp = lax.broadcasted_iota(jnp.int32, qt.shape, 0) // width
    return jnp.concatenate([jnp.where(grp == j, qt, 0.0) for j in range(n_stack)], axis=1)


def _softmax_step(s, m_ref, l_ref):
    m_old = m_ref[...]
    m_new = jnp.maximum(m_old, jnp.max(s, axis=0, keepdims=True))
    alpha = jnp.exp2(m_old - m_new)
    p = jnp.exp2(s - m_new)
    l_ref[...] = alpha * l_ref[...] + jnp.sum(p, axis=0, keepdims=True)
    m_ref[...] = m_new
    return alpha, p.astype(BF16)


def _rows_subln(x, g, lam_init):
    ms = jnp.mean(x * x, axis=0, keepdims=True)
    return x * lax.rsqrt(ms + NORM_EPS) * g * (1.0 - lam_init)


def _half_lane_mean_sq(o):
    lane = lax.broadcasted_iota(jnp.int32, o.shape, 1)
    sq = o * o
    lo = jnp.sum(jnp.where(lane < MB_DH, sq, 0.0), axis=1, keepdims=True)
    hi = jnp.sum(jnp.where(lane < MB_DH, 0.0, sq), axis=1, keepdims=True)
    return jnp.where(lane < MB_DH, lo, hi) * (1.0 / MB_DH)


def _subln(o_pair, g_pair, lam_init):
    return o_pair * lax.rsqrt(_half_lane_mean_sq(o_pair) + NORM_EPS) * g_pair * (1.0 - lam_init)


def _diff_attn_kernel(lam_ref, q_ref, k_ref, vt_ref, g_ref, o_ref, qs_ref, m_ref, l_ref, acc_ref,
                      *, tq, tk, lam_init):
    i = pl.program_id(1)
    q_start = i * tq
    qs_ref[...] = _stack_heads_t(q_ref[...], 2 * DA_HEADS, DA_DQK).astype(BF16)
    m_ref[...] = jnp.full(m_ref.shape, NEG_INF, F32)
    l_ref[...] = jnp.zeros(l_ref.shape, F32)
    acc_ref[...] = jnp.zeros(acc_ref.shape, F32)

    def step(c, masked):
        k0 = pl.multiple_of(c * tk, tk)
        s = _dot(k_ref[pl.ds(k0, tk), :], qs_ref[...])
        if masked:
            kpos = k0 + lax.broadcasted_iota(jnp.int32, s.shape, 0)
            qpos = q_start + (lax.broadcasted_iota(jnp.int32, s.shape, 1) & (tq - 1))
            s = jnp.where(kpos <= qpos, s, NEG_INF)
        alpha, pb = _softmax_step(s, m_ref, l_ref)
        for h in range(DA_HEADS):
            cols = slice(2 * h * tq, (2 * h + 2) * tq)
            pv = _dot(vt_ref[c, (h // 2) * LANES:(h // 2 + 1) * LANES, :], pb[:, cols])
            acc_ref[:, cols] = alpha[:, cols] * acc_ref[:, cols] + pv

    n_full = q_start // tk
    lax.fori_loop(0, n_full, lambda c, carry: (step(c, False), carry)[1], 0)
    step(n_full, True)

    lam = lam_ref[0]
    inv_l = 1.0 / l_ref[...]
    g = g_ref[...]
    slabs = []
    for pair in range(DA_HEADS // 2):
        halves = []
        for h in (2 * pair, 2 * pair + 1):
            c0, c1 = slice(2 * h * tq, (2 * h + 1) * tq), slice((2 * h + 1) * tq, (2 * h + 2) * tq)
            o = acc_ref[:, c0] * inv_l[:, c0] - lam * (acc_ref[:, c1] * inv_l[:, c1])
            halves.append(_rows_subln(o[(h % 2) * MB_DH:(h % 2 + 1) * MB_DH], g, lam_init))
        slabs.append(jnp.concatenate(halves, axis=0).T)
    o_ref[...] = jnp.concatenate(slabs, axis=1).astype(BF16)


def _diff_attn_prompt(lam, q, k, vt, g_rows, n_seq, seq, lam_init):
    tq, tk = DIFF_TQ, DIFF_TK
    nq = seq // tq
    cols = 2 * DA_HEADS * tq
    return pl.pallas_call(
        functools.partial(_diff_attn_kernel, tq=tq, tk=tk, lam_init=lam_init),
        grid=(n_seq, nq),
        in_specs=[pl.BlockSpec(memory_space=pltpu.SMEM),
                  pl.BlockSpec((tq, BRANCH_W), lambda b, i: (b * nq + i, 0)),
                  pl.BlockSpec((seq, BRANCH_W), lambda b, i: (b, 0)),
                  pl.BlockSpec((seq // tk, BRANCH_W, tk), lambda b, i: (b, 0, 0)),
                  pl.BlockSpec((MB_DH, tq), lambda b, i: (0, 0))],
        out_specs=pl.BlockSpec((tq, BRANCH_W), lambda b, i: (b * nq + i, 0)),
        out_shape=jax.ShapeDtypeStruct((n_seq * seq, BRANCH_W), BF16),
        scratch_shapes=[pltpu.VMEM((BRANCH_W, cols), BF16),
                        pltpu.VMEM((1, cols), F32), pltpu.VMEM((1, cols), F32),
                        pltpu.VMEM((LANES, cols), F32)],
        compiler_params=_params(("parallel", "arbitrary")), name="diff_attn_prompt",
    )(lam, q, k, vt, g_rows)


def _top3_bits(gs, n_valid):
    blk = lax.broadcasted_iota(jnp.int32, gs.shape, 0)
    blk_f = blk.astype(F32)
    gs = jnp.where(blk < n_valid, gs, NEG_INF)
    bits = jnp.zeros((1, gs.shape[1]), jnp.int32)
    for _ in range(MOBA_TOPK):
        mx = jnp.max(gs, axis=0, keepdims=True)
        idx = jnp.min(jnp.where(gs == mx, blk_f, 1e9), axis=0, keepdims=True).astype(jnp.int32)
        bits = bits | jnp.where(idx < n_valid, jnp.left_shift(1, idx), 0)
        gs = jnp.where(blk == idx, -3e38, gs)
    return bits


def _moba_attn_kernel(q_ref, k_ref, vt_ref, o_ref, qs_ref, km_ref, m_ref, l_ref, acc_ref, *, tq, n_blocks):
    i = pl.program_id(1)
    q_start = i * tq
    own = q_start // MOBA_BLOCK
    tk = MOBA_BLOCK

    @pl.when(i == 0)
    def _():
        for j in range(n_blocks):
            blk = k_ref[j * MOBA_BLOCK:(j + 1) * MOBA_BLOCK, :].astype(F32)
            km_ref[j:j + 1, :] = jnp.mean(blk, axis=0, keepdims=True)

    qt = _stack_heads_t(q_ref[...], MB_HEADS, MB_DH)
    qs_ref[...] = qt.astype(BF16)
    bits = _top3_bits(_dot_f32(km_ref[...], qt), own)
    m_ref[...] = jnp.full(m_ref.shape, NEG_INF, F32)
    l_ref[...] = jnp.zeros(l_ref.shape, F32)
    acc_ref[...] = jnp.zeros(acc_ref.shape, F32)

    def step(c, diagonal):
        k0 = pl.multiple_of(c * tk, tk)
        s = _dot(k_ref[pl.ds(k0, tk), :], qs_ref[...])
        if diagonal:
            kpos = k0 + lax.broadcasted_iota(jnp.int32, s.shape, 0)
            qpos = q_start + (lax.broadcasted_iota(jnp.int32, s.shape, 1) & (tq - 1))
            keep = kpos <= qpos
        else:
            keep = (jnp.right_shift(bits, c) & 1) == 1
        alpha, pb = _softmax_step(jnp.where(keep, s, NEG_INF), m_ref, l_ref)
        for h in range(MB_HEADS):
            cols = slice(h * tq, (h + 1) * tq)
            pv = _dot(vt_ref[c, (h // 2) * LANES:(h // 2 + 1) * LANES, :], pb[:, cols])
            acc_ref[:, cols] = alpha[:, cols] * acc_ref[:, cols] + pv

    step(own, True)
    lax.fori_loop(0, own, lambda c, carry: (step(c, False), carry)[1], 0)

    inv_l = 1.0 / l_ref[...]
    slabs = []
    for pair in range(MB_HEADS // 2):
        halves = []
        for h in (2 * pair, 2 * pair + 1):
            cols = slice(h * tq, (h + 1) * tq)
            o = acc_ref[:, cols] * inv_l[:, cols]
            halves.append(o[(h % 2) * MB_DH:(h % 2 + 1) * MB_DH])
        slabs.append(jnp.concatenate(halves, axis=0).T)
    o_ref[...] = jnp.concatenate(slabs, axis=1).astype(BF16)


def _moba_attn_prompt(q, k, vt, n_seq, seq):
    tq = MOBA_TQ
    nq = seq // tq
    n_blocks = seq // MOBA_BLOCK
    assert n_blocks <= 31 and tq <= MOBA_BLOCK
    cols = MB_HEADS * tq
    return pl.pallas_call(
        functools.partial(_moba_attn_kernel, tq=tq, n_blocks=n_blocks),
        grid=(n_seq, nq),
        in_specs=[pl.BlockSpec((tq, BRANCH_W), lambda b, i: (b * nq + i, 0)),
                  pl.BlockSpec((seq, BRANCH_W), lambda b, i: (b, 0)),
                  pl.BlockSpec((n_blocks, BRANCH_W, MOBA_BLOCK), lambda b, i: (b, 0, 0))],
        out_specs=pl.BlockSpec((tq, BRANCH_W), lambda b, i: (b * nq + i, 0)),
        out_shape=jax.ShapeDtypeStruct((n_seq * seq, BRANCH_W), BF16),
        scratch_shapes=[pltpu.VMEM((BRANCH_W, cols), BF16),
                        pltpu.VMEM((n_blocks, BRANCH_W), F32),
                        pltpu.VMEM((1, cols), F32), pltpu.VMEM((1, cols), F32),
                        pltpu.VMEM((LANES, cols), F32)],
        compiler_params=_params(("arbitrary", "arbitrary")), name="moba_attn_prompt",
    )(q, k, vt)


PAGES_PER_STEP = 16


def _cache_pages(cache):
    depth, pool, page, n_heads, dh = cache.shape
    return jnp.transpose(cache, (0, 1, 3, 4, 2)).reshape(depth, pool, n_heads * dh, page)


def _page_specs(n_pages_step, layer):
    def spec(j):
        return pl.BlockSpec((1, 1, BRANCH_W, PAGE_SIZE),
                            lambda b, p, pt, *_: (layer, pt[b, p * n_pages_step + j], 0, 0))
    return [spec(j) for j in range(n_pages_step)]


def _hi_lo_rows(x):
    hi = x.astype(BF16).astype(F32)
    return jnp.concatenate([hi, x - hi], axis=0).astype(BF16)


def _fold_rows(y):
    return y[0:SUBLANES] + y[SUBLANES:2 * SUBLANES]


def _scores_split(q16, page):
    k_hi, k_lo = _split_bf16(page)
    return _fold_rows(_dot(q16, k_hi) + _dot(q16, k_lo))


def _values_split(p16, page):
    v_hi, v_lo = _split_bf16(page)
    return _fold_rows(_dot_nt(p16, v_hi) + _dot_nt(p16, v_lo))


def _diff_decode_kernel(pt_ref, lam_ref, q_ref, kn_ref, vn_ref, g_ref, *refs, pps, lam_init):
    k_refs, v_refs = refs[:pps], refs[pps:2 * pps]
    o_ref, qs_ref, m_ref, l_ref, acc_ref = refs[2 * pps:]
    p_idx = pl.program_id(1)
    n_stack = 2 * DA_HEADS

    @pl.when(p_idx == 0)
    def _():
        grp = lax.broadcasted_iota(jnp.int32, (n_stack, BRANCH_W), 1) // DA_DQK
        row = lax.broadcasted_iota(jnp.int32, (n_stack, BRANCH_W), 0)
        qs = jnp.where(grp == row, jnp.broadcast_to(q_ref[0], (n_stack, BRANCH_W)), 0.0)
        qs_ref[...] = _hi_lo_rows(qs)
        m_ref[...] = jnp.sum(qs * kn_ref[0], axis=1, keepdims=True)
        l_ref[...] = jnp.ones(l_ref.shape, F32)
        acc_ref[...] = jnp.broadcast_to(vn_ref[0], acc_ref.shape)

    q16 = qs_ref[...]
    s = jnp.concatenate([_scores_split(q16, k_refs[j][0, 0]) for j in range(pps)], axis=1)
    m_old = m_ref[...]
    m_new = jnp.maximum(m_old, jnp.max(s, axis=1, keepdims=True))
    alpha = jnp.exp(m_old - m_new)
    p = jnp.exp(s - m_new)
    l_ref[...] = alpha * l_ref[...] + jnp.sum(p, axis=1, keepdims=True)
    m_ref[...] = m_new
    p16 = _hi_lo_rows(p)
    pv = _values_split(p16[:, 0:PAGE_SIZE], v_refs[0][0, 0])
    for j in range(1, pps):
        pv = pv + _values_split(p16[:, j * PAGE_SIZE:(j + 1) * PAGE_SIZE], v_refs[j][0, 0])
    acc_ref[...] = alpha * acc_ref[...] + pv

    @pl.when(p_idx == pl.num_programs(1) - 1)
    def _():
        lam = lam_ref[0]
        o = acc_ref[...] / l_ref[...]
        grp = lax.broadcasted_iota(jnp.int32, (1, BRANCH_W), 1) // MB_DH
        out = jnp.zeros((1, BRANCH_W), F32)
        for h in range(DA_HEADS):
            out = out + jnp.where(grp == h, o[2 * h:2 * h + 1] - lam * o[2 * h + 1:2 * h + 2], 0.0)
        lo, hi = out[:, 0:LANES], out[:, LANES:2 * LANES]
        g_pair = g_ref[...]
        o_ref[0] = jnp.concatenate([_subln(lo, g_pair, lam_init), _subln(hi, g_pair, lam_init)], axis=1)


def _diff_attn_decode(page_table, lam, q, k_new, v_new, g_pair, pages_k, pages_v, layer, lam_init):
    n_seq, n_pages = page_table.shape
    pps = min(PAGES_PER_STEP, n_pages)
    row3 = pl.BlockSpec((1, 1, BRANCH_W), lambda b, p, pt: (b, 0, 0))
    n_stack = 2 * DA_HEADS
    out = pl.pallas_call(
        functools.partial(_diff_decode_kernel, pps=pps, lam_init=lam_init),
        grid_spec=pltpu.PrefetchScalarGridSpec(
            num_scalar_prefetch=1, grid=(n_seq, n_pages // pps),
            in_specs=[pl.BlockSpec(memory_space=pltpu.SMEM), row3, row3, row3,
                      pl.BlockSpec((1, LANES), lambda b, p, pt: (0, 0))] +
                     _page_specs(pps, layer) + _page_specs(pps, layer),
            out_specs=row3,
            scratch_shapes=[pltpu.VMEM((2 * n_stack, BRANCH_W), BF16), pltpu.VMEM((n_stack, 1), F32),
                            pltpu.VMEM((n_stack, 1), F32), pltpu.VMEM((n_stack, BRANCH_W), F32)]),
        out_shape=jax.ShapeDtypeStruct((n_seq, 1, BRANCH_W), F32),
        compiler_params=_params(("arbitrary", "arbitrary")), name="diff_attn_decode",
    )(page_table, lam, q.reshape(n_seq, 1, BRANCH_W), k_new.reshape(n_seq, 1, BRANCH_W),
      v_new.reshape(n_seq, 1, BRANCH_W), g_pair, *([pages_k] * pps), *([pages_v] * pps))
    return out.reshape(n_seq, BRANCH_W)


def _moba_gate_kernel(pt_ref, q_ref, *refs, pps):
    k_refs, o_ref = refs[:pps], refs[pps]
    ppb = MOBA_BLOCK // PAGE_SIZE
    q_col = q_ref[0]
    for j in range(pps // ppb):
        tot = k_refs[ppb * j][0, 0]
        for t in range(1, ppb):
            tot = tot + k_refs[ppb * j + t][0, 0]
        prod = tot * q_col
        rows = [jnp.sum(prod[h * MB_DH:(h + 1) * MB_DH], axis=0, keepdims=True) for h in range(MB_HEADS)]
        rows.append(jnp.zeros((SUBLANES - MB_HEADS, PAGE_SIZE), F32))
        o_ref[0, j] = jnp.concatenate(rows, axis=0) * (1.0 / MOBA_BLOCK)


def _moba_select_kernel(g_ref, idx_ref, *, n_blocks):
    g = jnp.sum(g_ref[0], axis=1, keepdims=True)
    r = lax.broadcasted_iota(jnp.int32, g.shape, 0)
    r_f = r.astype(F32)
    out_row = lax.broadcasted_iota(jnp.int32, (SUBLANES, LANES), 0)
    out_lane = lax.broadcasted_iota(jnp.int32, (SUBLANES, LANES), 1)
    out = jnp.zeros((SUBLANES, LANES), jnp.int32)
    for h in range(MB_HEADS):
        gh = jnp.where((r & (SUBLANES - 1)) == h, g, -3e38)
        for t in range(MOBA_TOPK):
            mx = jnp.max(gh, axis=0, keepdims=True)
            idx = jnp.min(jnp.where(gh == mx, r_f, 1e9), axis=0, keepdims=True).astype(jnp.int32)
            gh = jnp.where(r == idx, -3e38, gh)
            out = jnp.where((out_row == t) & (out_lane == h), jnp.right_shift(idx, 3), out)
    idx_ref[0] = out


def _moba_decode_kernel(pt_ref, top_ref, q_ref, kn_ref, vn_ref, *refs, n_sel):
    k_refs, v_refs, o_ref = refs[:n_sel], refs[n_sel:2 * n_sel], refs[-1]
    qh = q_ref[0, 0]
    s_new = jnp.sum(qh * kn_ref[0, 0], axis=1, keepdims=True)
    q16 = _hi_lo_rows(jnp.broadcast_to(qh, (SUBLANES, MB_DH)))
    s = jnp.concatenate([_scores_split(q16, k_refs[j][0, 0]) for j in range(n_sel)], axis=1)[0:1]
    m = jnp.maximum(s_new, jnp.max(s, axis=1, keepdims=True))
    p_new = jnp.exp(s_new - m)
    p = jnp.exp(s - m)
    p16 = _hi_lo_rows(jnp.broadcast_to(p, (SUBLANES, n_sel * PAGE_SIZE)))
    pv = p_new * vn_ref[0, 0]
    for j in range(n_sel):
        pv = pv + _values_split(p16[:, j * PAGE_SIZE:(j + 1) * PAGE_SIZE], v_refs[j][0, 0])[0:1]
    o_ref[0, 0] = pv / (p_new + jnp.sum(p, axis=1, keepdims=True))


def _moba_attn_decode(page_table, q, k_new, v_new, pages_k, pages_v, layer):
    n_seq, n_pages = page_table.shape
    pps = min(PAGES_PER_STEP, n_pages)
    ppb = MOBA_BLOCK // PAGE_SIZE
    n_blocks = n_pages // ppb
    assert n_blocks >= MOBA_TOPK and pps % ppb == 0
    gate = pl.pallas_call(
        functools.partial(_moba_gate_kernel, pps=pps),
        grid_spec=pltpu.PrefetchScalarGridSpec(
            num_scalar_prefetch=1, grid=(n_seq, n_pages // pps),
            in_specs=[pl.BlockSpec((1, BRANCH_W, 1), lambda b, p, pt: (b, 0, 0))] + _page_specs(pps, layer),
            out_specs=pl.BlockSpec((1, pps // ppb, SUBLANES, PAGE_SIZE), lambda b, p, pt: (b, p, 0, 0))),
        out_shape=jax.ShapeDtypeStruct((n_seq, n_blocks, SUBLANES, PAGE_SIZE), F32),
        compiler_params=_params(("arbitrary", "arbitrary")), name="moba_gate_decode",
    )(page_table, q.reshape(n_seq, BRANCH_W, 1), *([pages_k] * pps))
    top = pl.pallas_call(
        functools.partial(_moba_select_kernel, n_blocks=n_blocks),
        grid=(n_seq,),
        in_specs=[pl.BlockSpec((1, n_blocks * SUBLANES, PAGE_SIZE), lambda b: (b, 0, 0))],
        out_specs=pl.BlockSpec((1, SUBLANES, LANES), lambda b: (b, 0, 0)),
        out_shape=jax.ShapeDtypeStruct((n_seq, SUBLANES, LANES), jnp.int32),
        compiler_params=_params(("parallel",)), name="moba_select_decode",
    )(gate.reshape(n_seq, n_blocks * SUBLANES, PAGE_SIZE))
    top = top[:, :MOBA_TOPK, :MB_HEADS].transpose(0, 2, 1).reshape(-1)
    head = pl.BlockSpec((1, 1, 1, MB_DH), lambda b, h, pt, tp: (b, h, 0, 0))
    n_sel = MOBA_TOPK * ppb

    def page(j):
        return pl.BlockSpec(
            (1, 1, MB_DH, PAGE_SIZE),
            lambda b, h, pt, tp: (layer, pt[b, tp[(b * MB_HEADS + h) * MOBA_TOPK + j // ppb] * ppb + j % ppb], h, 0))

    pages = [page(j) for j in range(n_sel)]
    per_head = lambda a: a.reshape(n_seq, MB_HEADS, 1, MB_DH)
    o = pl.pallas_call(
        functools.partial(_moba_decode_kernel, n_sel=n_sel),
        grid_spec=pltpu.PrefetchScalarGridSpec(
            num_scalar_prefetch=2, grid=(n_seq, MB_HEADS),
            in_specs=[head, head, head] + pages + pages,
            out_specs=head),
        out_shape=jax.ShapeDtypeStruct((n_seq, MB_HEADS, 1, MB_DH), F32),
        compiler_params=_params(("arbitrary", "arbitrary")), name="moba_attn_decode",
    )(page_table, top, per_head(q), per_head(k_new), per_head(v_new),
      *([pages_k] * len(pages)), *([pages_v] * len(pages)))
    return o.reshape(n_seq, BRANCH_W)


def _merge_kernel(x_ref, g_ref, oa_ref, ob_ref, oc_ref, od_ref, wgl_ref, wbr_ref, wout_ref, xo_ref):
    x = x_ref[...]
    hb = _rms(x, g_ref[...]).astype(BF16)
    merged = None
    for k, br in enumerate((oa_ref, ob_ref, oc_ref, od_ref)):
        gate = jax.nn.sigmoid(_dot(hb, wgl_ref[:, k * D_MODEL:(k + 1) * D_MODEL]))
        term = gate * _dot(br[...], wbr_ref[k])
        merged = term if merged is None else merged + term
    xo_ref[...] = x + _dot(merged.astype(BF16), wout_ref[...])


def _merge(x, g, oa, ob, oc, od, w_gl, w_br, w_out):
    n = x.shape[0]
    tm = min(512, n)
    row = lambda i: (i, 0)
    const2 = lambda i: (0, 0)
    br_spec = pl.BlockSpec((tm, BRANCH_W), row)
    return pl.pallas_call(
        _merge_kernel, grid=(n // tm,),
        in_specs=[pl.BlockSpec((tm, D_MODEL), row), pl.BlockSpec((1, D_MODEL), const2)] + [br_spec] * 4 +
                 [pl.BlockSpec((D_MODEL, N_BRANCH * D_MODEL), const2),
                  pl.BlockSpec((N_BRANCH, BRANCH_W, D_MODEL), lambda i: (0, 0, 0)),
                  pl.BlockSpec((D_MODEL, D_MODEL), const2)],
        out_specs=pl.BlockSpec((tm, D_MODEL), row),
        out_shape=jax.ShapeDtypeStruct((n, D_MODEL), F32),
        compiler_params=_params(("parallel",)), name="merge",
    )(x, g, oa, ob, oc, od, w_gl, w_br, w_out)


def _merge_decode_kernel(x_ref, g_ref, br_ref, wgl_ref, wbr_ref, wout_ref, xo_ref, acc_ref):
    k = pl.program_id(0)
    x = x_ref[...]
    h = _rms(x, g_ref[...])
    term = jax.nn.sigmoid(_dot_f32(h, wgl_ref[...])) * _dot_f32(br_ref[0], wbr_ref[0])

    @pl.when(k == 0)
    def _():
        acc_ref[...] = term

    @pl.when(k > 0)
    def _():
        acc_ref[...] += term

    @pl.when(k == N_BRANCH - 1)
    def _():
        xo_ref[...] = x + _dot_f32(acc_ref[...], wout_ref[...])


def _merge_decode(x, g, branches, w_gl, w_br, w_out):
    n = x.shape[0]
    const2 = lambda k: (0, 0)
    return pl.pallas_call(
        _merge_decode_kernel, grid=(N_BRANCH,),
        in_specs=[pl.BlockSpec((n, D_MODEL), const2), pl.BlockSpec((1, D_MODEL), const2),
                  pl.BlockSpec((1, n, BRANCH_W), lambda k: (k, 0, 0)),
                  pl.BlockSpec((D_MODEL, D_MODEL), lambda k: (0, k)),
                  pl.BlockSpec((1, BRANCH_W, D_MODEL), lambda k: (k, 0, 0)),
                  pl.BlockSpec((D_MODEL, D_MODEL), const2)],
        out_specs=pl.BlockSpec((n, D_MODEL), const2),
        out_shape=jax.ShapeDtypeStruct((n, D_MODEL), F32),
        scratch_shapes=[pltpu.VMEM((n, D_MODEL), F32)],
        compiler_params=_params(("arbitrary",)), name="merge_decode",
    )(x, g, branches, w_gl, w_br, w_out)


def _router_kernel(x_ref, g_ref, w_ref, b_ref, h_ref, comb_ref):
    h = _rms(x_ref[...], g_ref[...])
    h_ref[...] = h.astype(h_ref.dtype)
    lg = _dot_f32(h, w_ref[...]) + b_ref[...]
    lane = lax.broadcasted_iota(jnp.int32, lg.shape, 1)
    lane_f = lane.astype(F32)
    is_g = (lane >= N_EXPERTS) & (lane < N_EXPERTS + N_EGROUPS)
    mxg = jnp.max(jnp.where(is_g, lg, NEG_INF), axis=1, keepdims=True)
    gsel = jnp.min(jnp.where(is_g & (lg == mxg), lane_f, 1e9), axis=1, keepdims=True).astype(jnp.int32)
    gp = 1.0 / jnp.sum(jnp.where(is_g, jnp.exp(lg - mxg), 0.0), axis=1, keepdims=True)
    e_lo = (gsel - N_EXPERTS) * E_PER_GROUP
    in_g = (lane >= e_lo) & (lane < e_lo + E_PER_GROUP)
    v1 = jnp.max(jnp.where(in_g, lg, NEG_INF), axis=1, keepdims=True)
    i1 = jnp.min(jnp.where(in_g & (lg == v1), lane_f, 1e9), axis=1, keepdims=True).astype(jnp.int32)
    rest = in_g & (lane != i1)
    v2 = jnp.max(jnp.where(rest, lg, NEG_INF), axis=1, keepdims=True)
    i2 = jnp.min(jnp.where(rest & (lg == v2), lane_f, 1e9), axis=1, keepdims=True).astype(jnp.int32)
    e2 = jnp.exp(v2 - v1)
    w1 = gp / (1.0 + e2)
    w2 = gp * e2 / (1.0 + e2)
    comb = jnp.where(lane == i1, w1, 0.0) + jnp.where(lane == i2, w2, 0.0)
    comb = jnp.where(lane == ROUTE_ID_LANE, i1.astype(F32), comb)
    comb = jnp.where(lane == ROUTE_ID_LANE + 1, i2.astype(F32), comb)
    comb = jnp.where(lane == ROUTE_W_LANE, w1, comb)
    comb_ref[...] = jnp.where(lane == ROUTE_W_LANE + 1, w2, comb)


def _router(x, g, w_router, b_router, h_dtype):
    n = x.shape[0]
    tm = min(512, n)
    row = lambda i: (i, 0)
    const2 = lambda i: (0, 0)
    return pl.pallas_call(
        _router_kernel, grid=(n // tm,),
        in_specs=[pl.BlockSpec((tm, D_MODEL), row), pl.BlockSpec((1, D_MODEL), const2),
                  pl.BlockSpec((D_MODEL, LANES), const2), pl.BlockSpec((1, LANES), const2)],
        out_specs=[pl.BlockSpec((tm, D_MODEL), row), pl.BlockSpec((tm, LANES), row)],
        out_shape=[jax.ShapeDtypeStruct((n, D_MODEL), h_dtype), jax.ShapeDtypeStruct((n, LANES), F32)],
        compiler_params=_params(("parallel",)), name="router",
    )(x, g, w_router, b_router)


def _experts_kernel(x_ref, h_ref, comb_ref, wg_ref, wu_ref, wd_ref, gf_ref, xo_ref, acc_ref,
                    *, final_norm, precise):
    e = pl.program_id(1)
    dot = _dot_f32 if precise else _dot

    @pl.when(e == 0)
    def _():
        acc_ref[...] = x_ref[...]

    comb = comb_ref[...]
    lane = lax.broadcasted_iota(jnp.int32, comb.shape, 1)
    cw = jnp.sum(jnp.where(lane == e, comb, 0.0), axis=1, keepdims=True)
    hb = h_ref[...]
    a = jax.nn.silu(dot(hb, wg_ref[0])) * dot(hb, wu_ref[0])
    acc_ref[...] += dot((a * cw).astype(hb.dtype), wd_ref[0])

    @pl.when(e == pl.num_programs(1) - 1)
    def _():
        xo = acc_ref[...]
        xo_ref[...] = _rms(xo, gf_ref[...]) if final_norm else xo


def _experts(x, h, comb, w_gate, w_up, w_down, g_final, final_norm, precise):
    n = x.shape[0]
    tm = min(1024, n)
    row = lambda i, e: (i, 0)
    return pl.pallas_call(
        functools.partial(_experts_kernel, final_norm=final_norm, precise=precise),
        grid=(n // tm, N_EXPERTS),
        in_specs=[pl.BlockSpec((tm, D_MODEL), row), pl.BlockSpec((tm, D_MODEL), row),
                  pl.BlockSpec((tm, LANES), row),
                  pl.BlockSpec((1, D_MODEL, D_EXPERT), lambda i, e: (e, 0, 0)),
                  pl.BlockSpec((1, D_MODEL, D_EXPERT), lambda i, e: (e, 0, 0)),
                  pl.BlockSpec((1, D_EXPERT, D_MODEL), lambda i, e: (e, 0, 0)),
                  pl.BlockSpec((1, D_MODEL), lambda i, e: (0, 0))],
        out_specs=pl.BlockSpec((tm, D_MODEL), row),
        out_shape=jax.ShapeDtypeStruct((n, D_MODEL), F32),
        scratch_shapes=[pltpu.VMEM((tm, D_MODEL), F32)],
        compiler_params=_params(("parallel", "arbitrary")), name="experts",
    )(x, h, comb, w_gate, w_up, w_down, g_final)


def _routing_plan(comb, n):
    e_flat = comb[:, ROUTE_ID_LANE:ROUTE_ID_LANE + 2].astype(jnp.int32).reshape(-1)
    n_assign = e_flat.shape[0]
    n_pad = n_assign + N_EXPERTS * MOE_TM
    n_tiles = n_pad // MOE_TM
    n_tot = n + 2 * MOE_TM
    order = jnp.argsort(e_flat, stable=True).astype(jnp.int32)
    counts = jnp.sum(e_flat[:, None] == jnp.arange(N_EXPERTS, dtype=jnp.int32)[None, :], axis=0, dtype=jnp.int32)
    padded = (counts + MOE_TM - 1) // MOE_TM * MOE_TM
    pad_end = jnp.cumsum(padded)
    pad_start = pad_end - padded
    cnt_start = jnp.cumsum(counts) - counts
    n_used = (pad_end[-1] // MOE_TM).astype(jnp.int32).reshape(1)
    tile_row0 = jnp.arange(n_tiles, dtype=jnp.int32) * MOE_TM
    last_used = jnp.maximum(n_used[0] - 1, 0) * MOE_TM
    tile_expert = jnp.minimum(jnp.searchsorted(pad_end, jnp.minimum(tile_row0, last_used), side="right"),
                              N_EXPERTS - 1).astype(jnp.int32)
    row = jnp.arange(n_pad, dtype=jnp.int32)
    row_expert = tile_expert[row // MOE_TM]
    local = row - pad_start[row_expert]
    real = (local < counts[row_expert]) & (row < pad_end[-1])
    assign = order[jnp.clip(cnt_start[row_expert] + local, 0, n_assign - 1)]
    dst_row = jnp.where(real, (assign % 2) * n_tot + assign // 2, n + row % (2 * MOE_TM))
    return tile_expert, n_used, dst_row, n_tiles, n_tot


def _row_copies_start(idx_ref, base, src_of, dst_of, sem):
    def body(r, carry):
        i = idx_ref[base + r]
        pltpu.make_async_copy(src_of(r, i), dst_of(r, i), sem).start()
        return carry
    lax.fori_loop(0, MOE_TM, body, 0, unroll=8)


def _routed_experts_kernel(te_ref, nu_ref, dst_ref, x_hbm, g_ref, wg_ref, wu_ref, wd_ref, out_hbm,
                           xbuf, ybuf, gsem, ssem, *, n, n_tot):
    t = pl.program_id(0)
    n_used = nu_ref[0]
    slot = t % 2

    def token_of(row):
        return jnp.minimum(jnp.where(row >= n_tot, row - n_tot, row), n - 1)

    def gather_start(tile, s):
        _row_copies_start(dst_ref, tile * MOE_TM,
                          lambda r, row: x_hbm.at[pl.ds(token_of(row), 1), :],
                          lambda r, row: xbuf.at[s, pl.ds(r, 1), :], gsem.at[s])

    def gather_wait(s):
        pltpu.make_async_copy(x_hbm.at[pl.ds(0, MOE_TM), :], xbuf.at[s], gsem.at[s]).wait()

    def scatter_wait(s):
        pltpu.make_async_copy(ybuf.at[s], out_hbm.at[pl.ds(0, MOE_TM), :], ssem.at[s]).wait()

    @pl.when(t == 0)
    def _():
        ybuf[1] = jnp.zeros((MOE_TM, D_MODEL), F32)
        fills = [pltpu.make_async_copy(ybuf.at[1], out_hbm.at[pl.ds(base + j * MOE_TM, MOE_TM), :], ssem.at[1])
                 for base in (n, n_tot + n) for j in range((n_tot - n) // MOE_TM)]
        for f in fills:
            f.start()
        for f in fills:
            f.wait()

    @pl.when((t == 0) & (n_used > 0))
    def _():
        gather_start(0, 0)

    @pl.when(t < n_used)
    def _():
        gather_wait(slot)

        @pl.when(t + 1 < n_used)
        def _():
            gather_start(t + 1, 1 - slot)

        @pl.when(t >= 2)
        def _():
            scatter_wait(slot)

        hb = _rms(xbuf[slot], g_ref[...]).astype(BF16)
        a = jax.nn.silu(_dot(hb, wg_ref[0])) * _dot(hb, wu_ref[0])
        ybuf[slot] = _dot(a.astype(BF16), wd_ref[0])
        _row_copies_start(dst_ref, t * MOE_TM,
                          lambda r, row: ybuf.at[slot, pl.ds(r, 1), :],
                          lambda r, row: out_hbm.at[pl.ds(row, 1), :], ssem.at[slot])

        @pl.when(t == n_used - 1)
        def _():
            @pl.when(t >= 1)
            def _():
                scatter_wait(1 - slot)
            scatter_wait(slot)


def _routed_experts(x, comb, g, w_gate, w_up, w_down):
    n = x.shape[0]
    tile_expert, n_used, dst_row, n_tiles, n_tot = _routing_plan(comb, n)
    w_spec = lambda shape: pl.BlockSpec(shape, lambda t, te, nu, dst: (te[t], 0, 0))
    return pl.pallas_call(
        functools.partial(_routed_experts_kernel, n=n, n_tot=n_tot),
        grid_spec=pltpu.PrefetchScalarGridSpec(
            num_scalar_prefetch=3, grid=(n_tiles,),
            in_specs=[pl.BlockSpec(memory_space=pl.ANY),
                      pl.BlockSpec((1, D_MODEL), lambda t, te, nu, dst: (0, 0)),
                      w_spec((1, D_MODEL, D_EXPERT)), w_spec((1, D_MODEL, D_EXPERT)),
                      w_spec((1, D_EXPERT, D_MODEL))],
            out_specs=pl.BlockSpec(memory_space=pl.ANY),
            scratch_shapes=[pltpu.VMEM((2, MOE_TM, D_MODEL), F32), pltpu.VMEM((2, MOE_TM, D_MODEL), F32),
                            pltpu.SemaphoreType.DMA((2,)), pltpu.SemaphoreType.DMA((2,))]),
        out_shape=jax.ShapeDtypeStruct((2 * n_tot, D_MODEL), F32),
        compiler_params=_params(("arbitrary",)), name="routed_experts",
    )(tile_expert, n_used, dst_row, x, g, w_gate, w_up, w_down).reshape(2, n_tot, D_MODEL)


def _combine_kernel(x_ref, comb_ref, o0_ref, o1_ref, gf_ref, xo_ref, *, final_norm):
    comb = comb_ref[...]
    lane = lax.broadcasted_iota(jnp.int32, comb.shape, 1)
    w0 = jnp.sum(jnp.where(lane == ROUTE_W_LANE, comb, 0.0), axis=1, keepdims=True)
    w1 = jnp.sum(jnp.where(lane == ROUTE_W_LANE + 1, comb, 0.0), axis=1, keepdims=True)
    xo = x_ref[...] + w0 * o0_ref[0] + w1 * o1_ref[0]
    xo_ref[...] = _rms(xo, gf_ref[...]) if final_norm else xo


def _combine(x, comb, routed, g_final, final_norm):
    n = x.shape[0]
    tm = 512
    row = lambda i: (i, 0)
    return pl.pallas_call(
        functools.partial(_combine_kernel, final_norm=final_norm), grid=(n // tm,),
        in_specs=[pl.BlockSpec((tm, D_MODEL), row), pl.BlockSpec((tm, LANES), row),
                  pl.BlockSpec((1, tm, D_MODEL), lambda i: (0, i, 0)),
                  pl.BlockSpec((1, tm, D_MODEL), lambda i: (1, i, 0)),
                  pl.BlockSpec((1, D_MODEL), lambda i: (0, 0))],
        out_specs=pl.BlockSpec((tm, D_MODEL), row),
        out_shape=jax.ShapeDtypeStruct((n, D_MODEL), F32),
        compiler_params=_params(("parallel",)), name="moe_combine",
    )(x, comb, routed, routed, g_final)


def _layer_weights(l, W):
    w_in = W["w_in"][l]
    lam_init = 0.8 - 0.6 * math.exp(-0.3 * l)
    lam = (jnp.exp(jnp.sum(W["lambda_q1"][l] * W["lambda_k1"][l]))
           - jnp.exp(jnp.sum(W["lambda_q2"][l] * W["lambda_k2"][l])) + lam_init).reshape(1).astype(F32)
    tril = jnp.tril(jnp.ones((CHUNK, CHUNK), F32))
    ws, bs = W["chunk_ws"][l], W["chunk_b"][l]
    rep = BRANCH_W // CH_GROUPS
    router_w = jnp.concatenate(
        [W["router_e_w"][l].transpose(1, 0, 2).reshape(D_MODEL, N_EXPERTS), W["router_g_w"][l],
         jnp.zeros((D_MODEL, LANES - N_EXPERTS - N_EGROUPS), F32)], axis=1)
    router_b = jnp.concatenate(
        [W["router_e_b"][l].reshape(N_EXPERTS), W["router_g_b"][l],
         jnp.zeros((LANES - N_EXPERTS - N_EGROUPS,), F32)]).reshape(1, LANES)
    f32w = dict(
        w_qkv=w_in[:, :N_QKV_COLS], w_gl=w_in[:, N_QKV_COLS:], w_br=W["w_branch"][l], w_out=W["w_out"][l],
        w_gate=W["w_gate_e"][l].reshape(N_EXPERTS, D_MODEL, D_EXPERT),
        w_up=W["w_up_e"][l].reshape(N_EXPERTS, D_MODEL, D_EXPERT),
        w_down=W["w_down_e"][l].reshape(N_EXPERTS, D_EXPERT, D_MODEL))
    return dict(
        lam_init=lam_init, lam=lam,
        g_mix=W["norm_mix_g"][l].reshape(1, D_MODEL), g_ffn=W["norm_ffn_g"][l].reshape(1, D_MODEL),
        conv_w=W["conv_w"][l],
        wsc=(ws * tril).astype(BF16), bsf=jnp.repeat(bs.T, rep, axis=1),
        wdiag=jnp.repeat(ws[:, 0, 0], rep).reshape(1, BRANCH_W), b0=jnp.repeat(bs[:, 0], rep).reshape(1, BRANCH_W),
        g_sub=jnp.tile(W["subln_g"][l], LANES // MB_DH).reshape(1, LANES),
        g_rows=jnp.broadcast_to(W["subln_g"][l][:, None], (MB_DH, DIFF_TQ)),
        router_w=router_w, router_b=router_b,
        f32=f32w, bf16={k: v.astype(BF16) for k, v in f32w.items()},
    )


def _ffn(x, lw, g_final, final_norm, precise):
    w = lw["f32"] if precise else lw["bf16"]
    h, comb = _router(x, lw["g_ffn"], lw["router_w"], lw["router_b"], F32 if precise else BF16)
    if precise:
        return _experts(x, h, comb, w["w_gate"], w["w_up"], w["w_down"], g_final, final_norm, precise)
    routed = _routed_experts(x, comb, lw["g_ffn"], w["w_gate"], w["w_up"], w["w_down"])
    return _combine(x, comb, routed, g_final, final_norm)


def _run_prompt(x_prompt, LW, g_final):
    n_seq, seq, _ = x_prompt.shape
    x = x_prompt.reshape(n_seq * seq, D_MODEL)
    tabs = _rope_tables(seq, 0, 1)
    states = []
    for l, lw in enumerate(LW):
        w = lw["bf16"]
        (ak, av, mk, mv, conv, qa, ka, va, qm, km, vm, ob, od) = _inproj_prompt(
            x, lw["g_mix"], w["w_qkv"], tabs, lw["conv_w"], lw["wsc"], lw["bsf"], n_seq, seq)
        oa = _diff_attn_prompt(lw["lam"], qa, ka, va, lw["g_rows"], n_seq, seq, lw["lam_init"])
        oc = _moba_attn_prompt(qm, km, vm, n_seq, seq)
        x = _merge(x, lw["g_mix"], oa, ob, oc, od, w["w_gl"], w["w_br"], w["w_out"])
        x = _ffn(x, lw, g_final, l == len(LW) - 1, False)
        tok_major = lambda a: a.reshape(n_seq, DA_HEADS, 2 * DA_DQK, seq).transpose(0, 3, 1, 2)
        states.append((tok_major(ak), tok_major(av), tok_major(mk), tok_major(mv), conv[:, 6:8]))
    return x.reshape(n_seq, seq, D_MODEL), states


def _run_sample(x_sample, caches, state_conv, page_table, LW, g_final):
    n_seq = x_sample.shape[0]
    past_len = page_table.shape[1] * PAGE_SIZE
    x = x_sample.reshape(n_seq, D_MODEL)
    tabs = _rope_tables(n_seq, past_len, 0)
    pk_d, pv_d, pk_m, pv_m = [_cache_pages(c) for c in caches]
    states = []
    for l, lw in enumerate(LW):
        w = lw["f32"]
        s0, s1 = state_conv[l][:, 0], state_conv[l][:, 1]
        (ak, av, mk, mv, dv, z, qa, qm, ob, od) = _inproj_decode(
            x, lw["g_mix"], w["w_qkv"], tabs, lw["conv_w"], lw["wdiag"], lw["b0"], s0, s1)
        oa = _diff_attn_decode(page_table, lw["lam"], qa, ak, av, lw["g_sub"], pk_d, pv_d, l, lw["lam_init"])
        oc = _moba_attn_decode(page_table, qm, mk, mv, pk_m, pv_m, l)
        x = _merge_decode(x, lw["g_mix"], jnp.stack([oa, ob, oc, od]), w["w_gl"], w["w_br"], w["w_out"])
        x = _ffn(x, lw, g_final, l == len(LW) - 1, True)
        shp = (n_seq, 1, DA_HEADS, 2 * DA_DQK)
        states.append((ak.reshape(shp), av.reshape(shp), mk.reshape(shp), mv.reshape(shp),
                       jnp.stack([s1, z], axis=1), dv.reshape(n_seq, 1, BRANCH_W)))
    return x.reshape(n_seq, 1, D_MODEL), states


def kernel(x_prompt, x_sample, cache_k_diff, cache_v_diff, cache_k_moba, cache_v_moba, state_conv, page_table, norm_mix_g, w_in, lambda_q1, lambda_k1, lambda_q2, lambda_k2, subln_g, conv_w, chunk_ws, chunk_b, w_branch, w_out, norm_ffn_g, router_g_w, router_g_b, router_e_w, router_e_b, w_gate_e, w_up_e, w_down_e, norm_final_g):
    W = dict(norm_mix_g=norm_mix_g, w_in=w_in, lambda_q1=lambda_q1, lambda_k1=lambda_k1,
             lambda_q2=lambda_q2, lambda_k2=lambda_k2, subln_g=subln_g, conv_w=conv_w,
             chunk_ws=chunk_ws, chunk_b=chunk_b, w_branch=w_branch, w_out=w_out, norm_ffn_g=norm_ffn_g,
             router_g_w=router_g_w, router_g_b=router_g_b, router_e_w=router_e_w, router_e_b=router_e_b,
             w_gate_e=w_gate_e, w_up_e=w_up_e, w_down_e=w_down_e)
    depth = w_in.shape[0]
    LW = [_layer_weights(l, W) for l in range(depth)]
    g_final = norm_final_g.reshape(1, D_MODEL)
    y_prompt, sp = _run_prompt(x_prompt, LW, g_final)
    y_sample, ss = _run_sample(x_sample, (cache_k_diff, cache_v_diff, cache_k_moba, cache_v_moba),
                               state_conv, page_table, LW, g_final)
    stack = lambda st, k: jnp.stack([s[k] for s in st])
    return (y_prompt, y_sample,
            stack(sp, 0), stack(sp, 1), stack(sp, 2), stack(sp, 3), stack(sp, 4),
            stack(ss, 0), stack(ss, 1), stack(ss, 2), stack(ss, 3), stack(ss, 4), stack(ss, 5))
```
